```python
import math
import jax, jax.numpy as jnp
from jax import lax
import numpy as np

D_MODEL = 2048
BATCH = 2
SEQ = 16384
DEPTH = 1

D_MIX = D_MODEL
RWKV_WIDTH = D_MIX // 2
RWKV_HEAD_DIM = 64
RWKV_HEADS = RWKV_WIDTH // RWKV_HEAD_DIM
DECAY_RANK = 64
AICL_RANK = 64
GATE_RANK = 128
RWKV_COLS = 3 * RWKV_WIDTH + DECAY_RANK + AICL_RANK + GATE_RANK
RWKV_GN_EPS = 64e-5
DIFF_WIDTH = D_MIX - RWKV_WIDTH
DIFF_HEADS = 8
DIFF_HEAD_DIM = DIFF_WIDTH // (2 * DIFF_HEADS)
DIFF_COLS = 3 * DIFF_WIDTH
IN_COLS = RWKV_COLS + DIFF_COLS
Q_BLOCK = 128
N_BUCKETS = 32
MAX_DISTANCE = 128
SUBLN_EPS = 1e-5
N_GROUPS = 8
EXPERTS_PER_GROUP = 8
N_EXPERTS = N_GROUPS * EXPERTS_PER_GROUP
TOP_K_FINE = 2
D_EXPERT = D_MODEL // 2
NORM_EPS = 1e-6
MASK_VALUE = -1e30

kernel_name = "hymba_rwkv7_diffattn_hmoe_block"


def rms_norm(x, gain):
    xf = x.astype(jnp.float32)
    y = xf * lax.rsqrt(jnp.mean(xf * xf, axis=-1, keepdims=True) + NORM_EPS)
    return (y * gain.astype(jnp.float32)).astype(x.dtype)


def t5_bucket(rel):
    n = jnp.maximum(rel, 0)
    max_exact = N_BUCKETS // 2
    nf = jnp.maximum(n, 1).astype(jnp.float32)
    large = max_exact + (jnp.log(nf / max_exact) / math.log(MAX_DISTANCE / max_exact)
                         * (N_BUCKETS - max_exact)).astype(jnp.int32)
    large = jnp.minimum(large, N_BUCKETS - 1)
    return jnp.where(n < max_exact, n, large)


def rwkv7_time_mix(p, w_decay_up, decay_bias, w_aicl_up, aicl_bias, w_gate_up,
                   k_k, k_a, r_k, gn_w, gn_b):
    B, S, _ = p.shape
    H, N, W = RWKV_HEADS, RWKV_HEAD_DIM, RWKV_WIDTH
    f32 = jnp.float32
    r, k, v, xw, xa, xg = jnp.split(
        p, [W, 2 * W, 3 * W, 3 * W + DECAY_RANK, 3 * W + DECAY_RANK + AICL_RANK], axis=-1)
    w_log = -jax.nn.softplus(-(decay_bias + jnp.tanh(xw) @ w_decay_up).astype(f32)) - 0.5
    decay = jnp.exp(-jnp.exp(w_log))
    a = jax.nn.sigmoid((aicl_bias + xa @ w_aicl_up).astype(f32))
    g = jax.nn.sigmoid(xg) @ w_gate_up
    heads = lambda t: t.reshape(B, S, H, N)
    kk = heads((k * k_k).astype(f32))
    kk = kk / jnp.maximum(jnp.linalg.norm(kk, axis=-1, keepdims=True), 1e-12)
    k = k.astype(f32) * (1.0 + (a - 1.0) * k_a.astype(f32))
    r_h, k_h, v_h, w_h, a_h = heads(r.astype(f32)), heads(k), heads(v.astype(f32)), heads(decay), heads(a)
    tm = lambda t: jnp.moveaxis(t, 1, 0)

    def step(state, inp):
        r_t, w_t, k_t, v_t, kk_t, a_t = inp
        removal = jnp.einsum('bhvk,bhk->bhv', state, -kk_t)
        state = (state * w_t[:, :, None, :]
                 + removal[..., None] * (kk_t * a_t)[:, :, None, :]
                 + v_t[..., None] * k_t[:, :, None, :])
        return state, jnp.einsum('bhvk,bhk->bhv', state, r_t)

    s0 = jnp.zeros((B, H, N, N), f32)
    _, y = lax.scan(step, s0, (tm(r_h), tm(w_h), tm(k_h), tm(v_h), tm(kk), tm(a_h)))
    y = jnp.moveaxis(y, 0, 1)
    mu = jnp.mean(y, axis=-1, keepdims=True)
    var = jnp.mean(jnp.square(y - mu), axis=-1, keepdims=True)
    y = ((y - mu) * lax.rsqrt(var + RWKV_GN_EPS)).reshape(B, S, W) * gn_w + gn_b
    bonus = jnp.sum(r_h * k_h * r_k.astype(f32), axis=-1, keepdims=True) * v_h
    out = (y + bonus.reshape(B, S, W)) * g.astype(f32)
    return out.astype(p.dtype)


def diff_attention(p, rel_bias, lam_q1, lam_k1, lam_q2, lam_k2, subln_w, lambda_init):
    B, S, _ = p.shape
    H, d = DIFF_HEADS, DIFF_HEAD_DIM
    f32 = jnp.float32
    q, k, v = jnp.split(p, 3, axis=-1)
    q = (q * d ** -0.5).reshape(B, S, H, 2, d).transpose(0, 2, 3, 1, 4)
    k = k.reshape(B, S, H, 2, d).transpose(0, 2, 3, 1, 4)
    v = v.reshape(B, S, H, 2 * d).transpose(0, 2, 1, 3)
    lam = (jnp.exp(jnp.sum(lam_q1 * lam_k1).astype(f32))
           - jnp.exp(jnp.sum(lam_q2 * lam_k2).astype(f32)) + lambda_init)
    n_blk = S // Q_BLOCK
    q_blocks = q.reshape(B, H, 2, n_blk, Q_BLOCK, d).transpose(3, 0, 1, 2, 4, 5)
    k_pos = jnp.arange(S)

    def block(args):
        q_b, start = args
        s = jnp.einsum('bhmqd,bhmkd->bhmqk', q_b, k).astype(f32)
        rel = (start + jnp.arange(Q_BLOCK))[:, None] - k_pos[None, :]
        bias = jnp.moveaxis(rel_bias[t5_bucket(rel)], -1, 0).astype(f32)
        s = jnp.where(rel >= 0, s + bias[None, :, None], MASK_VALUE)
        pr = jax.nn.softmax(s, axis=-1)
        attn = pr[:, :, 0] - lam * pr[:, :, 1]
        return jnp.einsum('bhqk,bhke->bhqe', attn.astype(v.dtype), v)

    o = lax.map(block, (q_blocks, jnp.arange(n_blk) * Q_BLOCK))
    o = o.transpose(1, 0, 3, 2, 4).reshape(B, S, H, 2 * d).astype(f32)
    o = o * lax.rsqrt(jnp.mean(o * o, axis=-1, keepdims=True) + SUBLN_EPS)
    o = o * subln_w.astype(f32) * (1.0 - lambda_init)
    return o.reshape(B, S, H * 2 * d).astype(p.dtype)


def hierarchical_moe(h, w_coarse, b_coarse, w_fine, b_fine, w_exp_gate, w_exp_up, w_exp_down):
    B, S, D = h.shape
    T = B * S
    f32 = jnp.float32
    hf = h.reshape(T, D)
    coarse_logits = (hf @ w_coarse + b_coarse).astype(f32)
    coarse_p = jax.nn.softmax(coarse_logits, axis=-1)
    grp = jnp.argmax(coarse_logits, axis=-1)
    grp_p = jnp.take_along_axis(coarse_p, grp[:, None], axis=-1)
    fine_all = (hf @ w_fine + b_fine).astype(f32).reshape(T, N_GROUPS, EXPERTS_PER_GROUP)
    fine = jnp.take_along_axis(fine_all, grp[:, None, None], axis=1)[:, 0]
    top_v, top_i = lax.top_k(fine, TOP_K_FINE)
    weights = grp_p * jax.nn.softmax(top_v, axis=-1)
    expert = (grp[:, None] * EXPERTS_PER_GROUP + top_i).reshape(-1)
    order = jnp.argsort(expert)
    tok = order // TOP_K_FINE
    xs = hf[tok]
    sizes = jnp.bincount(expert, length=N_EXPERTS).astype(jnp.int32)
    gt = lax.ragged_dot(xs, w_exp_gate, sizes)
    up = lax.ragged_dot(xs, w_exp_up, sizes)
    ys = lax.ragged_dot(jax.nn.silu(gt) * up, w_exp_down, sizes)
    ys = ys.astype(f32) * weights.reshape(-1)[order][:, None]
    y = jnp.zeros((T, D), f32).at[tok].add(ys)
    return y.reshape(B, S, D).astype(h.dtype)


def setup_inputs(seed: int = 0) -> dict:
    key = jax.random.key(seed)
    ks = iter(jax.random.split(key, 48))
    nrm = lambda shape, scale: scale * jax.random.normal(next(ks), shape, jnp.float32)
    L, D, F = DEPTH, D_MODEL, D_EXPERT
    return {
        "x": nrm((BATCH, SEQ, D), 1.0),
        "c": nrm((BATCH, D), 1.0),
        "rel_bias": nrm((N_BUCKETS, DIFF_HEADS), 0.5),
        "w_mod": nrm((L, D, 6 * D), 0.5 * D ** -0.5),
        "b_mod": nrm((L, 6 * D), 0.02),
        "attn_pre_gain": 1.0 + nrm((L, D), 0.02),
        "attn_post_gain": 1.0 + nrm((L, D), 0.02),
        "w_in": nrm((L, D, IN_COLS), D ** -0.5),
        "shift_mu": jax.random.uniform(next(ks), (L, RWKV_COLS), jnp.float32),
        "w_decay_up": nrm((L, DECAY_RANK, RWKV_WIDTH), 0.1 * DECAY_RANK ** -0.5),
        "decay_bias": jax.random.uniform(next(ks), (L, RWKV_WIDTH), jnp.float32, -6.0, -1.0),
        "w_aicl_up": nrm((L, AICL_RANK, RWKV_WIDTH), AICL_RANK ** -0.5),
        "aicl_bias": nrm((L, RWKV_WIDTH), 0.1),
        "w_gate_up": nrm((L, GATE_RANK, RWKV_WIDTH), GATE_RANK ** -0.5),
        "k_k": 0.85 + nrm((L, RWKV_WIDTH), 0.02),
        "k_a": 1.0 + nrm((L, RWKV_WIDTH), 0.02),
        "r_k": nrm((L, RWKV_HEADS, RWKV_HEAD_DIM), 0.1),
        "gn_w": 1.0 + nrm((L, RWKV_WIDTH), 0.02),
        "gn_b": nrm((L, RWKV_WIDTH), 0.02),
        "lam_q1": nrm((L, DIFF_HEAD_DIM), 0.1),
        "lam_k1": nrm((L, DIFF_HEAD_DIM), 0.1),
        "lam_q2": nrm((L, DIFF_HEAD_DIM), 0.1),
        "lam_k2": nrm((L, DIFF_HEAD_DIM), 0.1),
        "subln_w": 1.0 + nrm((L, 2 * DIFF_HEAD_DIM), 0.02),
        "w_out": nrm((L, D_MIX, D), D_MIX ** -0.5),
        "ffn_pre_gain": 1.0 + nrm((L, D), 0.02),
        "ffn_post_gain": 1.0 + nrm((L, D), 0.02),
        "w_coarse": nrm((L, D, N_GROUPS), D ** -0.5),
        "b_coarse": nrm((L, N_GROUPS), 0.01),
        "w_fine": nrm((L, D, N_EXPERTS), D ** -0.5),
        "b_fine": nrm((L, N_EXPERTS), 0.01),
        "w_exp_gate": nrm((L, N_EXPERTS, D, F), D ** -0.5),
        "w_exp_up": nrm((L, N_EXPERTS, D, F), D ** -0.5),
        "w_exp_down": nrm((L, N_EXPERTS, F, D), F ** -0.5),
    }


def reference(x, c, rel_bias, w_mod, b_mod, attn_pre_gain, attn_post_gain, w_in, shift_mu,
              w_decay_up, decay_bias, w_aicl_up, aicl_bias, w_gate_up, k_k, k_a, r_k, gn_w, gn_b,
              lam_q1, lam_k1, lam_q2, lam_k2, subln_w, w_out, ffn_pre_gain, ffn_post_gain,
              w_coarse, b_coarse, w_fine, b_fine, w_exp_gate, w_exp_up, w_exp_down):
    for l in range(DEPTH):
        lambda_init = 0.8 - 0.6 * math.exp(-0.3 * l)
        mod = jax.nn.silu(c) @ w_mod[l] + b_mod[l]
        sh_a, sc_a, g_a, sh_f, sc_f, g_f = [m[:, None, :] for m in jnp.split(mod, 6, axis=-1)]

        h = rms_norm(x, attn_pre_gain[l]) * (1.0 + sc_a) + sh_a
        proj = h @ w_in[l]
        p_rw, p_da = proj[..., :RWKV_COLS], proj[..., RWKV_COLS:]
        prev = jnp.pad(p_rw, ((0, 0), (1, 0), (0, 0)))[:, :-1]
        p_rw = p_rw + (prev - p_rw) * shift_mu[l]
        o_rw = rwkv7_time_mix(p_rw, w_decay_up[l], decay_bias[l], w_aicl_up[l], aicl_bias[l],
                              w_gate_up[l], k_k[l], k_a[l], r_k[l], gn_w[l], gn_b[l])
        o_da = diff_attention(p_da, rel_bias, lam_q1[l], lam_k1[l], lam_q2[l], lam_k2[l],
                              subln_w[l], lambda_init)
        mix = jnp.concatenate([o_rw, o_da], axis=-1) @ w_out[l]
        x = x + g_a * rms_norm(mix, attn_post_gain[l])

        h = rms_norm(x, ffn_pre_gain[l]) * (1.0 + sc_f) + sh_f
        y = hierarchical_moe(h, w_coarse[l], b_coarse[l], w_fine[l], b_fine[l],
                             w_exp_gate[l], w_exp_up[l], w_exp_down[l])
        x = x + g_f * rms_norm(y, ffn_post_gain[l])
    return x
```

```python
import functools
import math

import jax
import jax.numpy as jnp
from jax import lax
from jax.experimental import pallas as pl
from jax.experimental.pallas import tpu as pltpu

F32 = jnp.float32
BF16 = jnp.bfloat16

RWKV_HEAD_DIM = 64
DECAY_RANK = 64
AICL_RANK = 64
GATE_RANK = 128
RWKV_GN_EPS = 64e-5
DIFF_HEADS = 8
DIFF_HEAD_DIM = 64
N_BUCKETS = 32
MAX_DISTANCE = 128
SUBLN_EPS = 1e-5
N_GROUPS = 8
EXPERTS_PER_GROUP = 8
N_EXPERTS = N_GROUPS * EXPERTS_PER_GROUP
NORM_EPS = 1e-6
MASK_VALUE = -1e30

LANES = 128
RWKV_CHUNK = 64
VMEM_LIMIT = 56 * 1024 * 1024


def _cparams(sem):
    return pltpu.CompilerParams(dimension_semantics=sem, vmem_limit_bytes=VMEM_LIMIT)


def _dot(a, b):
    return jnp.dot(a, b, preferred_element_type=F32)


def _dot_nt(a, b):
    return lax.dot_general(a, b, (((1,), (1,)), ((), ())), preferred_element_type=F32)


def _dot_tn(a, b):
    return lax.dot_general(a, b, (((0,), (0,)), ((), ())), preferred_element_type=F32)


def _mod_kernel(c_ref, w_ref, b_ref, o_ref):
    c = c_ref[...]
    s = c * jax.nn.sigmoid(c)
    o_ref[...] = _dot(s, w_ref[...]) + b_ref[...]


def _mod(c, w_mod, b_mod, tn=1024):
    B, D = c.shape
    N = w_mod.shape[1]
    rows = 8
    c8 = jnp.zeros((rows, D), F32).at[:B].set(c)
    out = pl.pallas_call(
        _mod_kernel,
        grid=(N // tn,),
        in_specs=[pl.BlockSpec((rows, D), lambda j: (0, 0)),
                  pl.BlockSpec((D, tn), lambda j: (0, j)),
                  pl.BlockSpec((1, tn), lambda j: (0, j))],
        out_specs=pl.BlockSpec((rows, tn), lambda j: (0, j)),
        out_shape=jax.ShapeDtypeStruct((rows, N), F32),
        compiler_params=_cparams(("arbitrary",)),
        name="mod",
    )(c8, w_mod, b_mod.reshape(1, N))
    return out[:B]


def _inproj_kernel(x_ref, gain_ref, sc_ref, sh_ref, w_ref, mu_ref, o_ref, carry_ref, *, shift, tiles_per_batch):
    i = pl.program_id(0)
    x = x_ref[...]
    ms = jnp.mean(x * x, axis=-1, keepdims=True)
    h = x * lax.rsqrt(ms + NORM_EPS) * gain_ref[...]
    h = h * (1.0 + sc_ref[0]) + sh_ref[0]
    p = _dot(h.astype(BF16), w_ref[...])
    if shift:
        tm = p.shape[0]

        @pl.when(i % tiles_per_batch == 0)
        def _():
            carry_ref[...] = jnp.zeros_like(carry_ref)

        prev = pltpu.roll(p, 1, axis=0)
        row = lax.broadcasted_iota(jnp.int32, p.shape, 0)
        prev = jnp.where(row == 0, carry_ref[0:1, :], prev)
        carry_ref[0:1, :] = p[tm - 1:tm, :]
        p = p + (prev - p) * mu_ref[...]
    o_ref[...] = p.astype(o_ref.dtype)


def _inproj(x2, gain, sc, sh, w_bf16, mu, *, shift, out_dtype, seq, tm):
    T, D = x2.shape
    N = w_bf16.shape[1]
    B = T // seq
    tiles_per_batch = seq // tm
    kern = functools.partial(_inproj_kernel, shift=shift, tiles_per_batch=tiles_per_batch)
    return pl.pallas_call(
        kern,
        grid=(T // tm,),
        in_specs=[pl.BlockSpec((tm, D), lambda i: (i, 0)),
                  pl.BlockSpec((1, D), lambda i: (0, 0)),
                  pl.BlockSpec((1, 1, D), lambda i: (i // tiles_per_batch, 0, 0)),
                  pl.BlockSpec((1, 1, D), lambda i: (i // tiles_per_batch, 0, 0)),
                  pl.BlockSpec((D, N), lambda i: (0, 0)),
                  pl.BlockSpec((1, N), lambda i: (0, 0))],
        out_specs=pl.BlockSpec((tm, N), lambda i: (i, 0)),
        out_shape=jax.ShapeDtypeStruct((T, N), out_dtype),
        scratch_shapes=[pltpu.VMEM((8, N), F32)],
        compiler_params=_cparams(("arbitrary",)),
        name="inproj_shift" if shift else "inproj",
    )(x2, gain.reshape(1, D), sc.reshape(B, 1, D), sh.reshape(B, 1, D), w_bf16, mu.reshape(1, N))


def _rwkv_kernel(p_ref, wdu_ref, db_ref, wau_ref, ab_ref, wgu_ref, kk_ref, ka_ref, rk_ref,
                 gnw_ref, gnb_ref, o_ref, s_ref, *, width):
    L = RWKV_CHUNK
    W = width
    n_pairs = W // LANES
    c = pl.program_id(1)

    @pl.when(c == 0)
    def _():
        s_ref[...] = jnp.zeros_like(s_ref)

    r = p_ref[:, 0:W]
    k = p_ref[:, W:2 * W]
    v = p_ref[:, 2 * W:3 * W]
    xw = p_ref[:, 3 * W:3 * W + DECAY_RANK]
    xa = p_ref[:, 3 * W + DECAY_RANK:3 * W + DECAY_RANK + AICL_RANK]
    xg = p_ref[:, 3 * W + DECAY_RANK + AICL_RANK:3 * W + DECAY_RANK + AICL_RANK + GATE_RANK]

    d = db_ref[...] + _dot(jnp.tanh(xw).astype(BF16), wdu_ref[...].astype(BF16))
    lw = -math.exp(-0.5) * jax.nn.sigmoid(d)
    a = jax.nn.sigmoid(ab_ref[...] + _dot(xa.astype(BF16), wau_ref[...].astype(BF16)))
    g = _dot(jax.nn.sigmoid(xg).astype(BF16), wgu_ref[...].astype(BF16))
    kkf = k * kk_ref[...]
    kmod = k * (1.0 + (a - 1.0) * ka_ref[...])

    ri = lax.broadcasted_iota(jnp.int32, (L, L), 0)
    ci = lax.broadcasted_iota(jnp.int32, (L, L), 1)
    tri = (ri >= ci).astype(BF16)
    lw_hi = lw.astype(BF16)
    rem = lw - lw_hi.astype(F32)
    lw_mid = rem.astype(BF16)
    lw_lo = (rem - lw_mid.astype(F32)).astype(BF16)
    cin = _dot(tri, lw_hi) + _dot(tri, lw_mid) + _dot(tri, lw_lo)
    cex = cin - lw
    c_last = cin[L - 1:L, :]
    e_in = jnp.exp(cin)
    e_ni = jnp.exp(-cin)
    e_ex = jnp.exp(cex)
    e_l = jnp.exp(c_last - cin)
    g_last = jnp.exp(c_last)

    lane = lax.broadcasted_iota(jnp.int32, (L, LANES), 1)
    lo = lane < RWKV_HEAD_DIM
    i2 = lax.broadcasted_iota(jnp.int32, (2 * L, LANES), 0)
    j2 = lax.broadcasted_iota(jnp.int32, (2 * L, LANES), 1)
    same_blk = (i2 // L) == (j2 // RWKV_HEAD_DIM)
    mask_strict = same_blk & (i2 > j2)
    mask_incl = same_blk & (i2 >= j2)
    eye = (i2 == j2).astype(F32)

    def stack(t):
        return jnp.concatenate([jnp.where(lo, t, 0.0), jnp.where(lo, 0.0, t)], axis=0)

    def dup(t):
        return jnp.concatenate([t, t], axis=0)

    def headsum(t):
        s_lo = jnp.sum(jnp.where(lo, t, 0.0), axis=-1, keepdims=True)
        s_hi = jnp.sum(jnp.where(lo, 0.0, t), axis=-1, keepdims=True)
        return jnp.where(lo, s_lo, s_hi)

    inv_n = 1.0 / RWKV_HEAD_DIM
    for hp in range(n_pairs):
        sl = slice(hp * LANES, (hp + 1) * LANES)
        kk_p = kkf[:, sl]
        nrm = jnp.maximum(jnp.sqrt(headsum(kk_p * kk_p)), 1e-12)
        kk_n = kk_p / nrm
        a_p = a[:, sl]
        kmod_p = kmod[:, sl]
        r_p = r[:, sl]
        v_p = v[:, sl]
        b_p = kk_n * a_p
        ag = stack(-kk_n * e_ex[:, sl])
        rg = stack(r_p * e_in[:, sl])
        bd = dup(b_p * e_ni[:, sl])
        kd = dup(kmod_p * e_ni[:, sl])
        bl = stack(b_p * e_l[:, sl])
        kl = stack(kmod_p * e_l[:, sl])
        v_st = stack(v_p).astype(BF16)

        lh = jnp.concatenate([ag, rg], axis=0).astype(BF16)
        rh = jnp.concatenate([bd, kd], axis=0).astype(BF16)
        a4 = _dot_nt(lh, rh)
        a_ab = jnp.where(mask_strict, a4[0:2 * L, 0:2 * L], 0.0)
        a_ak = jnp.where(mask_strict, a4[0:2 * L, 2 * L:4 * L], 0.0)
        a_rb = jnp.where(mask_incl, a4[2 * L:4 * L, 0:2 * L], 0.0)
        a_rk = jnp.where(mask_incl, a4[2 * L:4 * L, 2 * L:4 * L], 0.0)

        pw = a_ab
        tinv = eye + a_ab
        n_sq = int(math.log2(L)) - 1
        for _ in range(n_sq):
            pb = pw.astype(BF16)
            pw = _dot(pb, pb)
            tinv = tinv + _dot(tinv.astype(BF16), pw.astype(BF16))

        s0 = s_ref[hp]
        xs = _dot_nt(lh, s0.astype(BF16))
        u = _dot(tinv.astype(BF16), (xs[0:2 * L] + _dot(a_ak.astype(BF16), v_st)).astype(BF16))
        u_b = u.astype(BF16)
        y = xs[2 * L:4 * L] + _dot(a_rb.astype(BF16), u_b) + _dot(a_rk.astype(BF16), v_st)
        uv = jnp.concatenate([u_b, v_st], axis=0)
        bk = jnp.concatenate([bl, kl], axis=0).astype(BF16)
        s_ref[hp] = s0 * g_last[:, sl] + _dot_tn(uv, bk)

        mu = jnp.sum(y, axis=-1, keepdims=True) * inv_n
        dlt = jnp.where(same_blk, y - mu, 0.0)
        var = jnp.sum(dlt * dlt, axis=-1, keepdims=True) * inv_n
        yn = dlt * lax.rsqrt(var + RWKV_GN_EPS)
        y_p = (yn[0:L] + yn[L:2 * L]) * gnw_ref[:, sl] + gnb_ref[:, sl]
        bonus = headsum(r_p * kmod_p * rk_ref[:, sl]) * v_p
        o_ref[:, sl] = ((y_p + bonus) * g[:, sl]).astype(o_ref.dtype)


def _rwkv(p_rw, w_decay_up, decay_bias, w_aicl_up, aicl_bias, w_gate_up, k_k, k_a, r_k, gn_w, gn_b, *, seq):
    T, C = p_rw.shape
    W = w_decay_up.shape[1]
    B = T // seq
    L = RWKV_CHUNK
    nC = seq // L
    row = lambda t: t.reshape(1, W)
    full = lambda shape: pl.BlockSpec(shape, lambda b, c: (0,) * len(shape))
    return pl.pallas_call(
        functools.partial(_rwkv_kernel, width=W),
        grid=(B, nC),
        in_specs=[pl.BlockSpec((L, C), lambda b, c: (b * nC + c, 0)),
                  full((DECAY_RANK, W)), full((1, W)), full((AICL_RANK, W)), full((1, W)),
                  full((GATE_RANK, W)), full((1, W)), full((1, W)), full((1, W)), full((1, W)), full((1, W))],
        out_specs=pl.BlockSpec((L, W), lambda b, c: (b * nC + c, 0)),
        out_shape=jax.ShapeDtypeStruct((T, W), BF16),
        scratch_shapes=[pltpu.VMEM((W // LANES, LANES, LANES), F32)],
        compiler_params=_cparams(("arbitrary", "arbitrary")),
        name="rwkv",
    )(p_rw, w_decay_up, row(decay_bias), w_aicl_up, row(aicl_bias), w_gate_up, row(k_k), row(k_a),
      row(r_k), row(gn_w), row(gn_b))


def _t5_bucket(rel):
    n = jnp.maximum(rel, 0)
    max_exact = N_BUCKETS // 2
    nf = jnp.maximum(n, 1).astype(F32)
    large = max_exact + (jnp.log(nf / max_exact) / math.log(MAX_DISTANCE / max_exact)
                         * (N_BUCKETS - max_exact)).astype(jnp.int32)
    large = jnp.minimum(large, N_BUCKETS - 1)
    return jnp.where(n < max_exact, n, large)


def _attn_kernel(qi_tab, kj_tab, q_ref, k_ref, v_ref, bd_ref, bn_ref, lq1_ref, lk1_ref, lq2_ref, lk2_ref,
                 sw_ref, o_ref, m_ref, l_ref, acc_ref, *, tq, lambda_init):
    t = pl.program_id(2)
    qi = qi_tab[t]
    kj = kj_tab[t]

    @pl.when(kj == 0)
    def _():
        m_ref[...] = jnp.full_like(m_ref, MASK_VALUE)
        l_ref[...] = jnp.zeros_like(l_ref)
        acc_ref[...] = jnp.zeros_like(acc_ref)

    q = q_ref[...]
    lane = lax.broadcasted_iota(jnp.int32, q.shape, 1)
    zero = jnp.zeros_like(q)
    scale = DIFF_HEAD_DIM ** -0.5
    qs = jnp.concatenate([jnp.where(lane < DIFF_HEAD_DIM, q, zero),
                          jnp.where(lane < DIFF_HEAD_DIM, zero, q)], axis=0) * scale
    s = _dot_nt(qs.astype(BF16), k_ref[...])

    def update(sc):
        m_prev = m_ref[...]
        m_new = jnp.maximum(m_prev, jnp.max(sc, axis=-1, keepdims=True))
        alpha = jnp.exp(m_prev - m_new)
        p = jnp.exp(sc - m_new)
        l_ref[...] = alpha * l_ref[...] + jnp.sum(p, axis=-1, keepdims=True)
        acc_ref[...] = alpha * acc_ref[...] + _dot(p.astype(BF16), v_ref[...])
        m_ref[...] = m_new

    @pl.when(kj < qi - 1)
    def _():
        update(s)

    @pl.when(kj == qi - 1)
    def _():
        b = bn_ref[0]
        update(s + jnp.concatenate([b, b], axis=0))

    @pl.when(kj == qi)
    def _():
        b = bd_ref[0]
        update(s + jnp.concatenate([b, b], axis=0))
        l = l_ref[...]
        acc = acc_ref[...]
        o1 = acc[0:tq] / l[0:tq]
        o2 = acc[tq:2 * tq] / l[tq:2 * tq]
        lam = (jnp.exp(jnp.sum(lq1_ref[...] * lk1_ref[...], axis=-1, keepdims=True))
               - jnp.exp(jnp.sum(lq2_ref[...] * lk2_ref[...], axis=-1, keepdims=True)) + lambda_init)
        o = o1 - lam * o2
        o = o * lax.rsqrt(jnp.mean(o * o, axis=-1, keepdims=True) + SUBLN_EPS)
        o = o * sw_ref[...] * (1.0 - lambda_init)
        o_ref[...] = o.astype(o_ref.dtype)


def _diff_attention(p_da, rel_bias, lam_q1, lam_k1, lam_q2, lam_k2, subln_w, lambda_init, *, seq, tq):
    T, C = p_da.shape
    H = DIFF_HEADS
    B = T // seq
    nq = seq // tq
    hd2 = 2 * DIFF_HEAD_DIM
    far = rel_bias[N_BUCKETS - 1]
    ii = jnp.arange(tq)[:, None]
    jj = jnp.arange(tq)[None, :]
    rel_d = ii - jj
    rel_n = tq + ii - jj
    tbl = lambda rel: jnp.moveaxis(rel_bias[_t5_bucket(rel)] - far, -1, 0)
    bias_d = jnp.where(rel_d >= 0, tbl(rel_d), MASK_VALUE).astype(F32)
    bias_n = tbl(rel_n).astype(F32)
    pairs = [(a, b) for a in range(nq) for b in range(a + 1)]
    qi_tab = jnp.asarray([a for a, _ in pairs], jnp.int32)
    kj_tab = jnp.asarray([b for _, b in pairs], jnp.int32)
    vec = lambda t: t.reshape(1, -1)
    small = lambda n: pl.BlockSpec((1, n), lambda b, h, t, qt, kt: (0, 0))
    grid_spec = pltpu.PrefetchScalarGridSpec(
        num_scalar_prefetch=2,
        grid=(B, H, len(pairs)),
        in_specs=[pl.BlockSpec((tq, hd2), lambda b, h, t, qt, kt: (b * nq + qt[t], h)),
                  pl.BlockSpec((tq, hd2), lambda b, h, t, qt, kt: (b * nq + kt[t], H + h)),
                  pl.BlockSpec((tq, hd2), lambda b, h, t, qt, kt: (b * nq + kt[t], 2 * H + h)),
                  pl.BlockSpec((1, tq, tq), lambda b, h, t, qt, kt: (h, 0, 0)),
                  pl.BlockSpec((1, tq, tq), lambda b, h, t, qt, kt: (h, 0, 0)),
                  small(DIFF_HEAD_DIM), small(DIFF_HEAD_DIM), small(DIFF_HEAD_DIM), small(DIFF_HEAD_DIM),
                  small(hd2)],
        out_specs=pl.BlockSpec((tq, hd2), lambda b, h, t, qt, kt: (b * nq + qt[t], h)),
        scratch_shapes=[pltpu.VMEM((2 * tq, 1), F32), pltpu.VMEM((2 * tq, 1), F32),
                        pltpu.VMEM((2 * tq, hd2), F32)],
    )
    return pl.pallas_call(
        functools.partial(_attn_kernel, tq=tq, lambda_init=lambda_init),
        grid_spec=grid_spec,
        out_shape=jax.ShapeDtypeStruct((T, H * hd2), BF16),
        compiler_params=_cparams(("arbitrary", "arbitrary", "arbitrary")),
        name="diff_attn",
    )(qi_tab, kj_tab, p_da, p_da, p_da, bias_d, bias_n, vec(lam_q1), vec(lam_k1), vec(lam_q2), vec(lam_k2),
      vec(subln_w))


def _outproj_kernel(orw_ref, oda_ref, x_ref, w1_ref, w2_ref, pg_ref, ga_ref, fg_ref, scf_ref, shf_ref,
                    wr_ref, br_ref, x1_ref, h2_ref, route_ref):
    mix = _dot(orw_ref[...], w1_ref[...]) + _dot(oda_ref[...], w2_ref[...])
    y = mix * lax.rsqrt(jnp.mean(mix * mix, axis=-1, keepdims=True) + NORM_EPS) * pg_ref[...]
    x1 = x_ref[...] + ga_ref[0] * y
    x1_ref[...] = x1
    h2 = x1 * lax.rsqrt(jnp.mean(x1 * x1, axis=-1, keepdims=True) + NORM_EPS) * fg_ref[...]
    h2 = h2 * (1.0 + scf_ref[0]) + shf_ref[0]
    h2_ref[...] = h2.astype(h2_ref.dtype)

    logits = jnp.dot(h2, wr_ref[...], preferred_element_type=F32, precision=lax.Precision.HIGHEST) + br_ref[...]
    lane = lax.broadcasted_iota(jnp.int32, logits.shape, 1).astype(F32)
    big = float(LANES)
    neg = -jnp.inf
    cl = jnp.where(lane < N_GROUPS, logits, neg)
    cmax = jnp.max(cl, axis=-1, keepdims=True)
    grp = jnp.min(jnp.where(cl == cmax, lane, big), axis=-1, keepdims=True)
    grp_p = 1.0 / jnp.sum(jnp.exp(cl - cmax), axis=-1, keepdims=True)
    f_lo = N_GROUPS + EXPERTS_PER_GROUP * grp
    fl = jnp.where((lane >= f_lo) & (lane < f_lo + EXPERTS_PER_GROUP), logits, neg)
    v1 = jnp.max(fl, axis=-1, keepdims=True)
    i1 = jnp.min(jnp.where(fl == v1, lane, big), axis=-1, keepdims=True)
    fl2 = jnp.where(lane == i1, neg, fl)
    v2 = jnp.max(fl2, axis=-1, keepdims=True)
    i2 = jnp.min(jnp.where(fl2 == v2, lane, big), axis=-1, keepdims=True)
    e21 = jnp.exp(v2 - v1)
    w1 = grp_p / (1.0 + e21)
    w2 = w1 * e21
    route = jnp.where(lane == 0, i1 - N_GROUPS,
                      jnp.where(lane == 1, i2 - N_GROUPS,
                                jnp.where(lane == 2, w1, jnp.where(lane == 3, w2, 0.0))))
    route_ref[...] = route


def _outproj(o_rw, o_da, x2, w_out_bf16, post_gain, g_a, ffn_gain, sc_f, sh_f, w_route, b_route, *, seq, tm):
    T, D = x2.shape
    W1 = o_rw.shape[1]
    W2 = o_da.shape[1]
    B = T // seq
    tpb = seq // tm
    rowD = lambda: pl.BlockSpec((1, D), lambda i: (0, 0))
    perb = lambda: pl.BlockSpec((1, 1, D), lambda i: (i // tpb, 0, 0))
    r3 = lambda t: t.reshape(B, 1, D)
    return pl.pallas_call(
        _outproj_kernel,
        grid=(T // tm,),
        in_specs=[pl.BlockSpec((tm, W1), lambda i: (i, 0)),
                  pl.BlockSpec((tm, W2), lambda i: (i, 0)),
                  pl.BlockSpec((tm, D), lambda i: (i, 0)),
                  pl.BlockSpec((W1, D), lambda i: (0, 0)),
                  pl.BlockSpec((W2, D), lambda i: (1, 0)),
                  rowD(), perb(), rowD(), perb(), perb(),
                  pl.BlockSpec((D, LANES), lambda i: (0, 0)),
                  pl.BlockSpec((1, LANES), lambda i: (0, 0))],
        out_specs=[pl.BlockSpec((tm, D), lambda i: (i, 0)),
                   pl.BlockSpec((tm, D), lambda i: (i, 0)),
                   pl.BlockSpec((tm, LANES), lambda i: (i, 0))],
        out_shape=[jax.ShapeDtypeStruct((T, D), F32),
                   jax.ShapeDtypeStruct((T, D), BF16),
                   jax.ShapeDtypeStruct((T, LANES), F32)],
        compiler_params=_cparams(("arbitrary",)),
        name="outproj_router",
    )(o_rw, o_da, x2, w_out_bf16, w_out_bf16, post_gain.reshape(1, D), r3(g_a), ffn_gain.reshape(1, D),
      r3(sc_f), r3(sh_f), w_route, b_route)


def _expert_kernel(te_ref, nu_ref, x_ref, wg_ref, wu_ref, wd_ref, o_ref):
    i = pl.program_id(0)

    @pl.when(i < nu_ref[0])
    def _():
        x = x_ref[...]
        gt = _dot(x, wg_ref[0])
        up = _dot(x, wu_ref[0])
        act = (gt * jax.nn.sigmoid(gt) * up).astype(BF16)
        o_ref[...] = _dot(act, wd_ref[0]).astype(o_ref.dtype)


def _experts(xs, tile_expert, n_used, wg, wu, wd, *, tm):
    P, D = xs.shape
    F = wg.shape[2]
    n_tiles = P // tm
    row_idx = lambda i, te, nu: (jnp.minimum(i, nu[0] - 1), 0)
    grid_spec = pltpu.PrefetchScalarGridSpec(
        num_scalar_prefetch=2,
        grid=(n_tiles,),
        in_specs=[pl.BlockSpec((tm, D), row_idx),
                  pl.BlockSpec((1, D, F), lambda i, te, nu: (te[i], 0, 0)),
                  pl.BlockSpec((1, D, F), lambda i, te, nu: (te[i], 0, 0)),
                  pl.BlockSpec((1, F, D), lambda i, te, nu: (te[i], 0, 0))],
        out_specs=pl.BlockSpec((tm, D), row_idx),
    )
    return pl.pallas_call(
        _expert_kernel,
        grid_spec=grid_spec,
        out_shape=jax.ShapeDtypeStruct((P, D), BF16),
        compiler_params=_cparams(("arbitrary",)),
        name="experts",
    )(tile_expert, n_used, xs, wg, wu, wd)


def _combine_kernel(y0_ref, y1_ref, route_ref, x1_ref, pg_ref, gf_ref, o_ref):
    route = route_ref[...]
    w0 = route[:, 2:3]
    w1 = route[:, 3:4]
    y = y0_ref[...].astype(F32) * w0 + y1_ref[...].astype(F32) * w1
    yn = y * lax.rsqrt(jnp.mean(y * y, axis=-1, keepdims=True) + NORM_EPS) * pg_ref[...]
    o_ref[...] = x1_ref[...] + gf_ref[0] * yn


def _combine(y0, y1, route, x1, post_gain, g_f, *, seq, tm):
    T, D = x1.shape
    B = T // seq
    tpb = seq // tm
    return pl.pallas_call(
        _combine_kernel,
        grid=(T // tm,),
        in_specs=[pl.BlockSpec((tm, D), lambda i: (i, 0)),
                  pl.BlockSpec((tm, D), lambda i: (i, 0)),
                  pl.BlockSpec((tm, LANES), lambda i: (i, 0)),
                  pl.BlockSpec((tm, D), lambda i: (i, 0)),
                  pl.BlockSpec((1, D), lambda i: (0, 0)),
                  pl.BlockSpec((1, 1, D), lambda i: (i // tpb, 0, 0))],
        out_specs=pl.BlockSpec((tm, D), lambda i: (i, 0)),
        out_shape=jax.ShapeDtypeStruct((T, D), F32),
        compiler_params=_cparams(("arbitrary",)),
        name="combine",
    )(y0, y1, route, x1, post_gain.reshape(1, D), g_f.reshape(B, 1, D))


def _dispatch_plan(expert_flat, tm, n_tiles):
    n = expert_flat.shape[0]
    onehot = (expert_flat[:, None] == jnp.arange(N_EXPERTS, dtype=jnp.int32)[None, :]).astype(jnp.int32)
    csum = jnp.cumsum(onehot, axis=0)
    counts = csum[-1]
    rank = jnp.sum((csum - onehot) * onehot, axis=1)
    tiles_e = (counts + tm - 1) // tm
    tile_end = jnp.cumsum(tiles_e)
    tile_start = tile_end - tiles_e
    pos = tile_start[expert_flat] * tm + rank
    n_used = tile_end[-1]
    tile_ids = jnp.minimum(jnp.arange(n_tiles, dtype=jnp.int32), n_used - 1)
    tile_expert = jnp.minimum(jnp.searchsorted(tile_end, tile_ids, side="right"), N_EXPERTS - 1).astype(jnp.int32)
    row_token = jnp.zeros((n_tiles * tm,), jnp.int32).at[pos].set(jnp.arange(n, dtype=jnp.int32) // 2)
    return pos, row_token, tile_expert, n_used.astype(jnp.int32).reshape(1)


def _layer(x, c, rel_bias, w_mod, b_mod, attn_pre_gain, attn_post_gain, w_in, shift_mu, w_decay_up, decay_bias,
           w_aicl_up, aicl_bias, w_gate_up, k_k, k_a, r_k, gn_w, gn_b, lam_q1, lam_k1, lam_q2, lam_k2, subln_w,
           w_out, ffn_pre_gain, ffn_post_gain, w_coarse, b_coarse, w_fine, b_fine, w_exp_gate, w_exp_up,
           w_exp_down, *, layer_index, tm_proj, tq, tm_moe):
    B, S, D = x.shape
    T = B * S
    W = w_decay_up.shape[1]
    rw_cols = shift_mu.shape[0]
    lambda_init = 0.8 - 0.6 * math.exp(-0.3 * layer_index)
    x2 = x.reshape(T, D)

    mod = _mod(c, w_mod, b_mod)
    sh_a, sc_a, g_a, sh_f, sc_f, g_f = jnp.split(mod, 6, axis=-1)

    w_in_b = w_in.astype(BF16)
    p_rw = _inproj(x2, attn_pre_gain, sc_a, sh_a, w_in_b[:, :rw_cols], shift_mu,
                   shift=True, out_dtype=F32, seq=S, tm=tm_proj)
    p_da = _inproj(x2, attn_pre_gain, sc_a, sh_a, w_in_b[:, rw_cols:], jnp.zeros((w_in.shape[1] - rw_cols,), F32),
                   shift=False, out_dtype=BF16, seq=S, tm=tm_proj)

    o_rw = _rwkv(p_rw, w_decay_up, decay_bias, w_aicl_up, aicl_bias, w_gate_up, k_k, k_a, r_k.reshape(-1),
                 gn_w, gn_b, seq=S)
    o_da = _diff_attention(p_da, rel_bias, lam_q1, lam_k1, lam_q2, lam_k2, subln_w, lambda_init, seq=S, tq=tq)

    w_route = jnp.zeros((D, LANES), F32).at[:, :N_GROUPS].set(w_coarse).at[:, N_GROUPS:N_GROUPS + N_EXPERTS].set(w_fine)
    b_route = jnp.zeros((1, LANES), F32).at[0, :N_GROUPS].set(b_coarse).at[0, N_GROUPS:N_GROUPS + N_EXPERTS].set(b_fine)
    x1, h2, route = _outproj(o_rw, o_da, x2, w_out.astype(BF16), attn_post_gain, g_a, ffn_pre_gain, sc_f, sh_f,
                             w_route, b_route, seq=S, tm=tm_proj)

    expert_flat = route[:, 0:2].astype(jnp.int32).reshape(-1)
    n_tiles = (2 * T) // tm_moe + N_EXPERTS
    pos, row_token, tile_expert, n_used = _dispatch_plan(expert_flat, tm_moe, n_tiles)
    xs = h2[row_token]
    ys = _experts(xs, tile_expert, n_used, w_exp_gate.astype(BF16), w_exp_up.astype(BF16),
                  w_exp_down.astype(BF16), tm=tm_moe)
    pos2 = pos.reshape(T, 2)
    out = _combine(ys[pos2[:, 0]], ys[pos2[:, 1]], route, x1, ffn_post_gain, g_f, seq=S, tm=tm_proj)
    return out.reshape(B, S, D)


def kernel(x, c, rel_bias, w_mod, b_mod, attn_pre_gain, attn_post_gain, w_in, shift_mu, w_decay_up, decay_bias,
           w_aicl_up, aicl_bias, w_gate_up, k_k, k_a, r_k, gn_w, gn_b, lam_q1, lam_k1, lam_q2, lam_k2, subln_w,
           w_out, ffn_pre_gain, ffn_post_gain, w_coarse, b_coarse, w_fine, b_fine, w_exp_gate, w_exp_up,
           w_exp_down):
    depth = w_mod.shape[0]
    S = x.shape[1]
    tm_proj = min(256, S)
    tq = min(512, S)
    tm_moe = 512 if S >= 4096 else 128
    for l in range(depth):
        x = _layer(x, c, rel_bias, w_mod[l], b_mod[l], attn_pre_gain[l], attn_post_gain[l], w_in[l], shift_mu[l],
                   w_decay_up[l], decay_bias[l], w_aicl_up[l], aicl_bias[l], w_gate_up[l], k_k[l], k_a[l], r_k[l],
                   gn_w[l], gn_b[l], lam_q1[l], lam_k1[l], lam_q2[l], lam_k2[l], subln_w[l], w_out[l],
                   ffn_pre_gain[l], ffn_post_gain[l], w_coarse[l], b_coarse[l], w_fine[l], b_fine[l],
                   w_exp_gate[l], w_exp_up[l], w_exp_down[l], layer_index=l, tm_proj=tm_proj, tq=tq,
                   tm_moe=tm_moe)
    return x
```

```python
import functools
import math

import jax
import jax.numpy as jnp
from jax import lax
from jax.experimental import pallas as pl
from jax.experimental.pallas import tpu as pltpu

F32 = jnp.float32
BF16 = jnp.bfloat16

RWKV_HEAD_DIM = 64
DECAY_RANK = 64
AICL_RANK = 64
GATE_RANK = 128
RWKV_GN_EPS = 64e-5
DIFF_HEADS = 8
DIFF_HEAD_DIM = 64
N_BUCKETS = 32
MAX_DISTANCE = 128
SUBLN_EPS = 1e-5
N_GROUPS = 8
EXPERTS_PER_GROUP = 8
N_EXPERTS = N_GROUPS * EXPERTS_PER_GROUP
NORM_EPS = 1e-6
MASK_VALUE = -1e30
LOG2E = math.log2(math.e)

LANES = 128
RWKV_CHUNK = 64
VMEM_LIMIT = 56 * 1024 * 1024


def _cparams(sem):
    return pltpu.CompilerParams(dimension_semantics=sem, vmem_limit_bytes=VMEM_LIMIT)


def _dot(a, b):
    return jnp.dot(a, b, preferred_element_type=F32)


def _dot_nt(a, b):
    return lax.dot_general(a, b, (((1,), (1,)), ((), ())), preferred_element_type=F32)


def _dot_tn(a, b):
    return lax.dot_general(a, b, (((0,), (0,)), ((), ())), preferred_element_type=F32)


def _mod_kernel(c_ref, w_ref, b_ref, o_ref):
    c = c_ref[...]
    s = c * jax.nn.sigmoid(c)
    o_ref[...] = _dot(s, w_ref[...]) + b_ref[...]


def _mod(c, w_mod, b_mod, tn=1024):
    B, D = c.shape
    N = w_mod.shape[1]
    rows = 8
    c8 = jnp.zeros((rows, D), F32).at[:B].set(c)
    out = pl.pallas_call(
        _mod_kernel,
        grid=(N // tn,),
        in_specs=[pl.BlockSpec((rows, D), lambda j: (0, 0)),
                  pl.BlockSpec((D, tn), lambda j: (0, j)),
                  pl.BlockSpec((1, tn), lambda j: (0, j))],
        out_specs=pl.BlockSpec((rows, tn), lambda j: (0, j)),
        out_shape=jax.ShapeDtypeStruct((rows, N), F32),
        compiler_params=_cparams(("arbitrary",)),
        name="mod",
    )(c8, w_mod, b_mod.reshape(1, N))
    return out[:B]


def _inproj_kernel(x_ref, gain_ref, sc_ref, sh_ref, w_ref, mu_ref, o_ref, carry_ref, *, shift, tiles_per_batch):
    i = pl.program_id(0)
    x = x_ref[...]
    ms = jnp.mean(x * x, axis=-1, keepdims=True)
    h = x * lax.rsqrt(ms + NORM_EPS) * gain_ref[...]
    h = h * (1.0 + sc_ref[0]) + sh_ref[0]
    p = _dot(h.astype(BF16), w_ref[...])
    if shift:
        tm = p.shape[0]

        @pl.when(i % tiles_per_batch == 0)
        def _():
            carry_ref[...] = jnp.zeros_like(carry_ref)

        prev = pltpu.roll(p, 1, axis=0)
        row = lax.broadcasted_iota(jnp.int32, p.shape, 0)
        prev = jnp.where(row == 0, carry_ref[0:1, :], prev)
        carry_ref[0:1, :] = p[tm - 1:tm, :]
        p = p + (prev - p) * mu_ref[...]
    o_ref[...] = p.astype(o_ref.dtype)


def _inproj(x2, gain, sc, sh, w_bf16, mu, *, shift, out_dtype, seq, tm):
    T, D = x2.shape
    N = w_bf16.shape[1]
    B = T // seq
    tiles_per_batch = seq // tm
    kern = functools.partial(_inproj_kernel, shift=shift, tiles_per_batch=tiles_per_batch)
    return pl.pallas_call(
        kern,
        grid=(T // tm,),
        in_specs=[pl.BlockSpec((tm, D), lambda i: (i, 0)),
                  pl.BlockSpec((1, D), lambda i: (0, 0)),
                  pl.BlockSpec((1, 1, D), lambda i: (i // tiles_per_batch, 0, 0)),
                  pl.BlockSpec((1, 1, D), lambda i: (i // tiles_per_batch, 0, 0)),
                  pl.BlockSpec((D, N), lambda i: (0, 0)),
                  pl.BlockSpec((1, N), lambda i: (0, 0))],
        out_specs=pl.BlockSpec((tm, N), lambda i: (i, 0)),
        out_shape=jax.ShapeDtypeStruct((T, N), out_dtype),
        scratch_shapes=[pltpu.VMEM((8, N), F32)],
        compiler_params=_cparams(("arbitrary",)),
        name="inproj_shift" if shift else "inproj",
    )(x2, gain.reshape(1, D), sc.reshape(B, 1, D), sh.reshape(B, 1, D), w_bf16, mu.reshape(1, N))


def _rwkv_kernel(p_ref, wdu_ref, db_ref, wau_ref, ab_ref, wgu_ref, kk_ref, ka_ref, rk_ref,
                 gnw_ref, gnb_ref, o_ref, s_ref, *, width):
    L = RWKV_CHUNK
    W = width
    n_pairs = W // LANES
    c = pl.program_id(1)

    @pl.when(c == 0)
    def _():
        s_ref[...] = jnp.zeros_like(s_ref)

    r = p_ref[:, 0:W]
    k = p_ref[:, W:2 * W]
    v = p_ref[:, 2 * W:3 * W]
    xw = p_ref[:, 3 * W:3 * W + DECAY_RANK]
    xa = p_ref[:, 3 * W + DECAY_RANK:3 * W + DECAY_RANK + AICL_RANK]
    xg = p_ref[:, 3 * W + DECAY_RANK + AICL_RANK:3 * W + DECAY_RANK + AICL_RANK + GATE_RANK]

    d = db_ref[...] + _dot(jnp.tanh(xw).astype(BF16), wdu_ref[...].astype(BF16))
    lw = -math.exp(-0.5) * jax.nn.sigmoid(d)
    a = jax.nn.sigmoid(ab_ref[...] + _dot(xa.astype(BF16), wau_ref[...].astype(BF16)))
    g = _dot(jax.nn.sigmoid(xg).astype(BF16), wgu_ref[...].astype(BF16))
    kkf = k * kk_ref[...]
    kmod = k * (1.0 + (a - 1.0) * ka_ref[...])

    ri = lax.broadcasted_iota(jnp.int32, (L, L), 0)
    ci = lax.broadcasted_iota(jnp.int32, (L, L), 1)
    tri = (ri >= ci).astype(BF16)
    lw_hi = lw.astype(BF16)
    rem = lw - lw_hi.astype(F32)
    lw_mid = rem.astype(BF16)
    lw_lo = (rem - lw_mid.astype(F32)).astype(BF16)
    cin = _dot(tri, lw_hi) + _dot(tri, lw_mid) + _dot(tri, lw_lo)
    cex = cin - lw
    c_last = cin[L - 1:L, :]
    e_in = jnp.exp(cin)
    e_ni = jnp.exp(-cin)
    e_ex = jnp.exp(cex)
    e_l = jnp.exp(c_last - cin)
    g_last = jnp.exp(c_last)

    lane = lax.broadcasted_iota(jnp.int32, (L, LANES), 1)
    lo = lane < RWKV_HEAD_DIM
    i2 = lax.broadcasted_iota(jnp.int32, (2 * L, LANES), 0)
    j2 = lax.broadcasted_iota(jnp.int32, (2 * L, LANES), 1)
    same_blk = (i2 // L) == (j2 // RWKV_HEAD_DIM)
    mask_strict = same_blk & (i2 > j2)
    mask_incl = same_blk & (i2 >= j2)
    eye = (i2 == j2).astype(F32)

    def stack(t):
        return jnp.concatenate([jnp.where(lo, t, 0.0), jnp.where(lo, 0.0, t)], axis=0)

    def dup(t):
        return jnp.concatenate([t, t], axis=0)

    def headsum(t):
        s_lo = jnp.sum(jnp.where(lo, t, 0.0), axis=-1, keepdims=True)
        s_hi = jnp.sum(jnp.where(lo, 0.0, t), axis=-1, keepdims=True)
        return jnp.where(lo, s_lo, s_hi)

    inv_n = 1.0 / RWKV_HEAD_DIM
    pairs = range(n_pairs)
    sls = [slice(hp * LANES, (hp + 1) * LANES) for hp in pairs]
    lh, rh, v_st, bk = [], [], [], []
    for sl in sls:
        kk_p = kkf[:, sl]
        nrm = jnp.maximum(jnp.sqrt(headsum(kk_p * kk_p)), 1e-12)
        kn = kk_p / nrm
        bp = kn * a[:, sl]
        ag = stack(-kn * e_ex[:, sl])
        rg = stack(r[:, sl] * e_in[:, sl])
        bd = dup(bp * e_ni[:, sl])
        kd = dup(kmod[:, sl] * e_ni[:, sl])
        bl = stack(bp * e_l[:, sl])
        kl = stack(kmod[:, sl] * e_l[:, sl])
        v_st.append(stack(v[:, sl]).astype(BF16))
        lh.append(jnp.concatenate([ag, rg], axis=0).astype(BF16))
        rh.append(jnp.concatenate([bd, kd], axis=0).astype(BF16))
        bk.append(jnp.concatenate([bl, kl], axis=0).astype(BF16))
    a4 = [_dot_nt(lh[i], rh[i]) for i in pairs]
    a_ab = [jnp.where(mask_strict, a4[i][0:2 * L, 0:2 * L], 0.0) for i in pairs]
    a_ak = [jnp.where(mask_strict, a4[i][0:2 * L, 2 * L:4 * L], 0.0).astype(BF16) for i in pairs]
    a_rb = [jnp.where(mask_incl, a4[i][2 * L:4 * L, 0:2 * L], 0.0).astype(BF16) for i in pairs]
    a_rk = [jnp.where(mask_incl, a4[i][2 * L:4 * L, 2 * L:4 * L], 0.0).astype(BF16) for i in pairs]

    pw = a_ab
    tinv = [eye + a_ab[i] for i in pairs]
    for _ in range(int(math.log2(L)) - 1):
        pb = [pw[i].astype(BF16) for i in pairs]
        pw = [_dot(pb[i], pb[i]) for i in pairs]
        tinv = [tinv[i] + _dot(tinv[i].astype(BF16), pw[i].astype(BF16)) for i in pairs]

    s0 = [s_ref[i] for i in pairs]
    xs = [_dot_nt(lh[i], s0[i].astype(BF16)) for i in pairs]
    akv = [_dot(a_ak[i], v_st[i]) for i in pairs]
    u_b = [_dot(tinv[i].astype(BF16), (xs[i][0:2 * L] + akv[i]).astype(BF16)).astype(BF16) for i in pairs]
    y = [xs[i][2 * L:4 * L] + _dot(a_rb[i], u_b[i]) + _dot(a_rk[i], v_st[i]) for i in pairs]
    for i in pairs:
        uv = jnp.concatenate([u_b[i], v_st[i]], axis=0)
        s_ref[i] = s0[i] * g_last[:, sls[i]] + _dot_tn(uv, bk[i])

    for i, sl in enumerate(sls):
        mu = jnp.sum(y[i], axis=-1, keepdims=True) * inv_n
        dlt = jnp.where(same_blk, y[i] - mu, 0.0)
        var = jnp.sum(dlt * dlt, axis=-1, keepdims=True) * inv_n
        yn = dlt * lax.rsqrt(var + RWKV_GN_EPS)
        y_p = (yn[0:L] + yn[L:2 * L]) * gnw_ref[:, sl] + gnb_ref[:, sl]
        bonus = headsum(r[:, sl] * kmod[:, sl] * rk_ref[:, sl]) * v[:, sl]
        o_ref[:, sl] = ((y_p + bonus) * g[:, sl]).astype(o_ref.dtype)


def _rwkv(p_rw, w_decay_up, decay_bias, w_aicl_up, aicl_bias, w_gate_up, k_k, k_a, r_k, gn_w, gn_b, *, seq):
    T, C = p_rw.shape
    W = w_decay_up.shape[1]
    B = T // seq
    L = RWKV_CHUNK
    nC = seq // L
    row = lambda t: t.reshape(1, W)
    full = lambda shape: pl.BlockSpec(shape, lambda b, c: (0,) * len(shape))
    return pl.pallas_call(
        functools.partial(_rwkv_kernel, width=W),
        grid=(B, nC),
        in_specs=[pl.BlockSpec((L, C), lambda b, c: (b * nC + c, 0)),
                  full((DECAY_RANK, W)), full((1, W)), full((AICL_RANK, W)), full((1, W)),
                  full((GATE_RANK, W)), full((1, W)), full((1, W)), full((1, W)), full((1, W)), full((1, W))],
        out_specs=pl.BlockSpec((L, W), lambda b, c: (b * nC + c, 0)),
        out_shape=jax.ShapeDtypeStruct((T, W), BF16),
        scratch_shapes=[pltpu.VMEM((W // LANES, LANES, LANES), F32)],
        compiler_params=_cparams(("arbitrary", "arbitrary")),
        name="rwkv",
    )(p_rw, w_decay_up, row(decay_bias), w_aicl_up, row(aicl_bias), w_gate_up, row(k_k), row(k_a),
      row(r_k), row(gn_w), row(gn_b))


def _t5_bucket(rel):
    n = jnp.maximum(rel, 0)
    max_exact = N_BUCKETS // 2
    nf = jnp.maximum(n, 1).astype(F32)
    large = max_exact + (jnp.log(nf / max_exact) / math.log(MAX_DISTANCE / max_exact)
                         * (N_BUCKETS - max_exact)).astype(jnp.int32)
    large = jnp.minimum(large, N_BUCKETS - 1)
    return jnp.where(n < max_exact, n, large)


def _attn_kernel(q_ref, k_ref, v_ref, bd_ref, bn_ref, lq1_ref, lk1_ref, lq2_ref, lk2_ref,
                 sw_ref, o_ref, qs_ref, m_ref, l_ref, acc_ref, *, tq, rsub, lambda_init):
    qi = pl.program_id(2)
    tk = tq
    n_rep = tk // LANES

    q = q_ref[...]
    lane = lax.broadcasted_iota(jnp.int32, q.shape, 1)
    zero = jnp.zeros_like(q)
    qs_ref[0:tq, :] = jnp.where(lane < DIFF_HEAD_DIM, q, zero)
    qs_ref[tq:2 * tq, :] = jnp.where(lane < DIFF_HEAD_DIM, zero, q)
    m_ref[...] = jnp.full_like(m_ref, MASK_VALUE)
    l_ref[...] = jnp.zeros_like(l_ref)
    acc_ref[...] = jnp.zeros_like(acc_ref)

    def tile(kj, bias_ref):
        koff = pl.multiple_of(kj * tk, tk)
        k = k_ref[pl.ds(koff, tk), :]
        v = v_ref[pl.ds(koff, tk), :]
        subs = [slice(sb * rsub, (sb + 1) * rsub) for sb in range(2 * tq // rsub)]
        m_prev = [m_ref[rows, :] for rows in subs]
        l_prev = [l_ref[rows, :] for rows in subs]
        acc_prev = [acc_ref[rows, :] for rows in subs]
        s_all = [_dot_nt(qs_ref[rows, :], k) for rows in subs]
        m_out, l_out, alphas, ps = [], [], [], []
        for sb, s in enumerate(s_all):
            if bias_ref is not None:
                b0 = (sb * rsub) % tq
                s = s + bias_ref[0, b0:b0 + rsub, :]
            m_new = jnp.maximum(m_prev[sb], jnp.max(s, axis=-1, keepdims=True))
            alpha = jnp.exp2(m_prev[sb] - m_new)
            p = jnp.exp2(s - pltpu.repeat(m_new, n_rep, axis=1))
            m_out.append(m_new)
            l_out.append(alpha * l_prev[sb] + jnp.sum(p, axis=-1, keepdims=True))
            alphas.append(alpha)
            ps.append(p.astype(BF16))
        pv = [_dot(p, v) for p in ps]
        for sb, rows in enumerate(subs):
            m_ref[rows, :] = m_out[sb]
            l_ref[rows, :] = l_out[sb]
            acc_ref[rows, :] = alphas[sb] * acc_prev[sb] + pv[sb]

    def far_tile(kj, carry):
        tile(kj, None)
        return carry

    lax.fori_loop(0, jnp.maximum(qi - 1, 0), far_tile, 0)

    @pl.when(qi >= 1)
    def _():
        tile(qi - 1, bn_ref)

    tile(qi, bd_ref)
    l = l_ref[...]
    acc = acc_ref[...]
    o1 = acc[0:tq] / l[0:tq]
    o2 = acc[tq:2 * tq] / l[tq:2 * tq]
    lam = (jnp.exp(jnp.sum(lq1_ref[...] * lk1_ref[...], axis=-1, keepdims=True))
           - jnp.exp(jnp.sum(lq2_ref[...] * lk2_ref[...], axis=-1, keepdims=True)) + lambda_init)
    o = o1 - lam * o2
    o = o * lax.rsqrt(jnp.mean(o * o, axis=-1, keepdims=True) + SUBLN_EPS)
    o = o * sw_ref[...] * (1.0 - lambda_init)
    o_ref[...] = o.astype(o_ref.dtype)


def _diff_attention(p_da, rel_bias, lam_q1, lam_k1, lam_q2, lam_k2, subln_w, lambda_init, *, seq, tq, rsub):
    T, C = p_da.shape
    H = DIFF_HEADS
    B = T // seq
    nq = seq // tq
    hd2 = 2 * DIFF_HEAD_DIM
    far = rel_bias[N_BUCKETS - 1]
    ii = jnp.arange(tq)[:, None]
    jj = jnp.arange(tq)[None, :]
    rel_d = ii - jj
    rel_n = tq + ii - jj
    tbl = lambda rel: jnp.moveaxis(rel_bias[_t5_bucket(rel)] - far, -1, 0) * LOG2E
    bias_d = jnp.where(rel_d >= 0, tbl(rel_d), MASK_VALUE).astype(F32)
    bias_n = tbl(rel_n).astype(F32)
    vec = lambda t: t.reshape(1, -1)
    small = lambda n: pl.BlockSpec((1, n), lambda b, h, i: (0, 0))
    return pl.pallas_call(
        functools.partial(_attn_kernel, tq=tq, rsub=rsub, lambda_init=lambda_init),
        grid=(B, H, nq),
        in_specs=[pl.BlockSpec((tq, hd2), lambda b, h, i: (b * nq + i, h)),
                  pl.BlockSpec((seq, hd2), lambda b, h, i: (b, H + h)),
                  pl.BlockSpec((seq, hd2), lambda b, h, i: (b, 2 * H + h)),
                  pl.BlockSpec((1, tq, tq), lambda b, h, i: (h, 0, 0)),
                  pl.BlockSpec((1, tq, tq), lambda b, h, i: (h, 0, 0)),
                  small(DIFF_HEAD_DIM), small(DIFF_HEAD_DIM), small(DIFF_HEAD_DIM), small(DIFF_HEAD_DIM),
                  small(hd2)],
        out_specs=pl.BlockSpec((tq, hd2), lambda b, h, i: (b * nq + i, h)),
        out_shape=jax.ShapeDtypeStruct((T, H * hd2), BF16),
        scratch_shapes=[pltpu.VMEM((2 * tq, hd2), BF16), pltpu.VMEM((2 * tq, LANES), F32),
                        pltpu.VMEM((2 * tq, LANES), F32), pltpu.VMEM((2 * tq, hd2), F32)],
        compiler_params=_cparams(("arbitrary", "arbitrary", "arbitrary")),
        name="diff_attn",
    )(p_da, p_da, p_da, bias_d, bias_n, vec(lam_q1), vec(lam_k1), vec(lam_q2), vec(lam_k2), vec(subln_w))


def _outproj_kernel(orw_ref, oda_ref, x_ref, w1_ref, w2_ref, pg_ref, ga_ref, fg_ref, scf_ref, shf_ref,
                    wr_ref, br_ref, x1_ref, h2_ref, route_ref):
    mix = _dot(orw_ref[...], w1_ref[...]) + _dot(oda_ref[...], w2_ref[...])
    y = mix * lax.rsqrt(jnp.mean(mix * mix, axis=-1, keepdims=True) + NORM_EPS) * pg_ref[...]
    x1 = x_ref[...] + ga_ref[0] * y
    x1_ref[...] = x1
    h2 = x1 * lax.rsqrt(jnp.mean(x1 * x1, axis=-1, keepdims=True) + NORM_EPS) * fg_ref[...]
    h2 = h2 * (1.0 + scf_ref[0]) + shf_ref[0]
    h2_ref[...] = h2.astype(h2_ref.dtype)

    logits = jnp.dot(h2, wr_ref[...], preferred_element_type=F32, precision=lax.Precision.HIGHEST) + br_ref[...]
    lane = lax.broadcasted_iota(jnp.int32, logits.shape, 1).astype(F32)
    big = float(LANES)
    neg = -jnp.inf
    cl = jnp.where(lane < N_GROUPS, logits, neg)
    cmax = jnp.max(cl, axis=-1, keepdims=True)
    grp = jnp.min(jnp.where(cl == cmax, lane, big), axis=-1, keepdims=True)
    grp_p = 1.0 / jnp.sum(jnp.exp(cl - cmax), axis=-1, keepdims=True)
    f_lo = N_GROUPS + EXPERTS_PER_GROUP * grp
    fl = jnp.where((lane >= f_lo) & (lane < f_lo + EXPERTS_PER_GROUP), logits, neg)
    v1 = jnp.max(fl, axis=-1, keepdims=True)
    i1 = jnp.min(jnp.where(fl == v1, lane, big), axis=-1, keepdims=True)
    fl2 = jnp.where(lane == i1, neg, fl)
    v2 = jnp.max(fl2, axis=-1, keepdims=True)
    i2 = jnp.min(jnp.where(fl2 == v2, lane, big), axis=-1, keepdims=True)
    e21 = jnp.exp(v2 - v1)
    w1 = grp_p / (1.0 + e21)
    w2 = w1 * e21
    route = jnp.where(lane == 0, i1 - N_GROUPS,
                      jnp.where(lane == 1, i2 - N_GROUPS,
                                jnp.where(lane == 2, w1, jnp.where(lane == 3, w2, 0.0))))
    route_ref[...] = route


def _outproj(o_rw, o_da, x2, w_out_bf16, post_gain, g_a, ffn_gain, sc_f, sh_f, w_route, b_route, *, seq, tm):
    T, D = x2.shape
    W1 = o_rw.shape[1]
    W2 = o_da.shape[1]
    B = T // seq
    tpb = seq // tm
    rowD = lambda: pl.BlockSpec((1, D), lambda i: (0, 0))
    perb = lambda: pl.BlockSpec((1, 1, D), lambda i: (i // tpb, 0, 0))
    r3 = lambda t: t.reshape(B, 1, D)
    return pl.pallas_call(
        _outproj_kernel,
        grid=(T // tm,),
        in_specs=[pl.BlockSpec((tm, W1), lambda i: (i, 0)),
                  pl.BlockSpec((tm, W2), lambda i: (i, 0)),
                  pl.BlockSpec((tm, D), lambda i: (i, 0)),
                  pl.BlockSpec((W1, D), lambda i: (0, 0)),
                  pl.BlockSpec((W2, D), lambda i: (1, 0)),
                  rowD(), perb(), rowD(), perb(), perb(),
                  pl.BlockSpec((D, LANES), lambda i: (0, 0)),
                  pl.BlockSpec((1, LANES), lambda i: (0, 0))],
        out_specs=[pl.BlockSpec((tm, D), lambda i: (i, 0)),
                   pl.BlockSpec((tm, D), lambda i: (i, 0)),
                   pl.BlockSpec((tm, LANES), lambda i: (i, 0))],
        out_shape=[jax.ShapeDtypeStruct((T, D), F32),
                   jax.ShapeDtypeStruct((T, D), BF16),
                   jax.ShapeDtypeStruct((T, LANES), F32)],
        compiler_params=_cparams(("arbitrary",)),
        name="outproj_router",
    )(o_rw, o_da, x2, w_out_bf16, w_out_bf16, post_gain.reshape(1, D), r3(g_a), ffn_gain.reshape(1, D),
      r3(sc_f), r3(sh_f), w_route, b_route)


def _new_expert(te_ref, i):
    return (i == 0) | (te_ref[i] != te_ref[jnp.maximum(i - 1, 0)])


def _expert_up_kernel(te_ref, nu_ref, x_ref, wg_ref, wu_ref, o_ref, wgb_ref, wub_ref):
    i = pl.program_id(1)

    @pl.when(i < nu_ref[0])
    def _():
        @pl.when(_new_expert(te_ref, i))
        def _():
            wgb_ref[...] = wg_ref[0].astype(BF16)
            wub_ref[...] = wu_ref[0].astype(BF16)

        x = x_ref[...]
        gt = _dot(x, wgb_ref[...])
        up = _dot(x, wub_ref[...])
        o_ref[...] = (gt * jax.nn.sigmoid(gt) * up).astype(o_ref.dtype)


def _expert_down_kernel(te_ref, nu_ref, a_ref, wd_ref, o_ref, wdb_ref):
    i = pl.program_id(1)

    @pl.when(i < nu_ref[0])
    def _():
        @pl.when(_new_expert(te_ref, i))
        def _():
            wdb_ref[...] = wd_ref[0].astype(BF16)

        o_ref[...] = _dot(a_ref[...], wdb_ref[...]).astype(o_ref.dtype)


def _experts(xs, tile_expert, n_used, wg, wu, wd, *, tm, n_split=2):
    P, D = xs.shape
    F = wg.shape[2]
    n_tiles = P // tm
    fh = F // n_split
    dh = D // n_split
    row = lambda j, i, te, nu: jnp.minimum(i, nu[0] - 1)
    act = pl.pallas_call(
        _expert_up_kernel,
        grid_spec=pltpu.PrefetchScalarGridSpec(
            num_scalar_prefetch=2,
            grid=(n_split, n_tiles),
            in_specs=[pl.BlockSpec((tm, D), lambda j, i, te, nu: (row(j, i, te, nu), 0)),
                      pl.BlockSpec((1, D, fh), lambda j, i, te, nu: (te[i], 0, j)),
                      pl.BlockSpec((1, D, fh), lambda j, i, te, nu: (te[i], 0, j))],
            out_specs=pl.BlockSpec((tm, fh), lambda j, i, te, nu: (row(j, i, te, nu), j)),
            scratch_shapes=[pltpu.VMEM((D, fh), BF16), pltpu.VMEM((D, fh), BF16)],
        ),
        out_shape=jax.ShapeDtypeStruct((P, F), BF16),
        compiler_params=_cparams(("arbitrary", "arbitrary")),
        name="experts_up",
    )(tile_expert, n_used, xs, wg, wu)
    return pl.pallas_call(
        _expert_down_kernel,
        grid_spec=pltpu.PrefetchScalarGridSpec(
            num_scalar_prefetch=2,
            grid=(n_split, n_tiles),
            in_specs=[pl.BlockSpec((tm, F), lambda j, i, te, nu: (row(j, i, te, nu), 0)),
                      pl.BlockSpec((1, F, dh), lambda j, i, te, nu: (te[i], 0, j))],
            out_specs=pl.BlockSpec((tm, dh), lambda j, i, te, nu: (row(j, i, te, nu), j)),
            scratch_shapes=[pltpu.VMEM((F, dh), BF16)],
        ),
        out_shape=jax.ShapeDtypeStruct((P, D), BF16),
        compiler_params=_cparams(("arbitrary", "arbitrary")),
        name="experts_down",
    )(tile_expert, n_used, act, wd)


def _combine_kernel(y0_ref, y1_ref, route_ref, x1_ref, pg_ref, gf_ref, o_ref):
    route = route_ref[...]
    w0 = route[:, 2:3]
    w1 = route[:, 3:4]
    y = y0_ref[...].astype(F32) * w0 + y1_ref[...].astype(F32) * w1
    yn = y * lax.rsqrt(jnp.mean(y * y, axis=-1, keepdims=True) + NORM_EPS) * pg_ref[...]
    o_ref[...] = x1_ref[...] + gf_ref[0] * yn


def _combine(y0, y1, route, x1, post_gain, g_f, *, seq, tm):
    T, D = x1.shape
    B = T // seq
    tpb = seq // tm
    return pl.pallas_call(
        _combine_kernel,
        grid=(T // tm,),
        in_specs=[pl.BlockSpec((tm, D), lambda i: (i, 0)),
                  pl.BlockSpec((tm, D), lambda i: (i, 0)),
                  pl.BlockSpec((tm, LANES), lambda i: (i, 0)),
                  pl.BlockSpec((tm, D), lambda i: (i, 0)),
                  pl.BlockSpec((1, D), lambda i: (0, 0)),
                  pl.BlockSpec((1, 1, D), lambda i: (i // tpb, 0, 0))],
        out_specs=pl.BlockSpec((tm, D), lambda i: (i, 0)),
        out_shape=jax.ShapeDtypeStruct((T, D), F32),
        compiler_params=_cparams(("arbitrary",)),
        name="combine",
    )(y0, y1, route, x1, post_gain.reshape(1, D), g_f.reshape(B, 1, D))


def _dispatch_plan(expert_flat, tm, n_tiles):
    n = expert_flat.shape[0]
    onehot = (expert_flat[:, None] == jnp.arange(N_EXPERTS, dtype=jnp.int32)[None, :]).astype(jnp.int32)
    csum = jnp.cumsum(onehot, axis=0)
    counts = csum[-1]
    rank = jnp.sum((csum - onehot) * onehot, axis=1)
    tiles_e = (counts + tm - 1) // tm
    tile_end = jnp.cumsum(tiles_e)
    tile_start = tile_end - tiles_e
    pos = tile_start[expert_flat] * tm + rank
    n_used = tile_end[-1]
    tile_ids = jnp.minimum(jnp.arange(n_tiles, dtype=jnp.int32), n_used - 1)
    tile_expert = jnp.minimum(jnp.searchsorted(tile_end, tile_ids, side="right"), N_EXPERTS - 1).astype(jnp.int32)
    row_token = jnp.zeros((n_tiles * tm,), jnp.int32).at[pos].set(jnp.arange(n, dtype=jnp.int32) // 2)
    return pos, row_token, tile_expert, n_used.astype(jnp.int32).reshape(1)


def _layer(x, c, rel_bias, w_mod, b_mod, attn_pre_gain, attn_post_gain, w_in, shift_mu, w_decay_up, decay_bias,
           w_aicl_up, aicl_bias, w_gate_up, k_k, k_a, r_k, gn_w, gn_b, lam_q1, lam_k1, lam_q2, lam_k2, subln_w,
           w_out, ffn_pre_gain, ffn_post_gain, w_coarse, b_coarse, w_fine, b_fine, w_exp_gate, w_exp_up,
           w_exp_down, *, layer_index, tm_proj, tq, rsub_attn, tm_moe):
    B, S, D = x.shape
    T = B * S
    W = w_decay_up.shape[1]
    rw_cols = shift_mu.shape[0]
    lambda_init = 0.8 - 0.6 * math.exp(-0.3 * layer_index)
    x2 = x.reshape(T, D)

    mod = _mod(c, w_mod, b_mod)
    sh_a, sc_a, g_a, sh_f, sc_f, g_f = jnp.split(mod, 6, axis=-1)

    q_cols = DIFF_HEADS * 2 * DIFF_HEAD_DIM
    q_scale = DIFF_HEAD_DIM ** -0.5 * LOG2E
    w_rw_b = w_in[:, :rw_cols].astype(BF16)
    w_da_b = jnp.concatenate([w_in[:, rw_cols:rw_cols + q_cols] * q_scale, w_in[:, rw_cols + q_cols:]],
                             axis=1).astype(BF16)
    p_rw = _inproj(x2, attn_pre_gain, sc_a, sh_a, w_rw_b, shift_mu,
                   shift=True, out_dtype=F32, seq=S, tm=tm_proj)
    p_da = _inproj(x2, attn_pre_gain, sc_a, sh_a, w_da_b, jnp.zeros((w_in.shape[1] - rw_cols,), F32),
                   shift=False, out_dtype=BF16, seq=S, tm=tm_proj)

    o_rw = _rwkv(p_rw, w_decay_up, decay_bias, w_aicl_up, aicl_bias, w_gate_up, k_k, k_a, r_k.reshape(-1),
                 gn_w, gn_b, seq=S)
    o_da = _diff_attention(p_da, rel_bias, lam_q1, lam_k1, lam_q2, lam_k2, subln_w, lambda_init, seq=S, tq=tq,
                           rsub=rsub_attn)

    w_route = jnp.zeros((D, LANES), F32).at[:, :N_GROUPS].set(w_coarse).at[:, N_GROUPS:N_GROUPS + N_EXPERTS].set(w_fine)
    b_route = jnp.zeros((1, LANES), F32).at[0, :N_GROUPS].set(b_coarse).at[0, N_GROUPS:N_GROUPS + N_EXPERTS].set(b_fine)
    x1, h2, route = _outproj(o_rw, o_da, x2, w_out.astype(BF16), attn_post_gain, g_a, ffn_pre_gain, sc_f, sh_f,
                             w_route, b_route, seq=S, tm=tm_proj)

    expert_flat = route[:, 0:2].astype(jnp.int32).reshape(-1)
    n_tiles = (2 * T) // tm_moe + N_EXPERTS
    pos, row_token, tile_expert, n_used = _dispatch_plan(expert_flat, tm_moe, n_tiles)
    xs = h2[row_token]
    ys = _experts(xs, tile_expert, n_used, w_exp_gate, w_exp_up, w_exp_down, tm=tm_moe)
    pos2 = pos.reshape(T, 2)
    out = _combine(ys[pos2[:, 0]], ys[pos2[:, 1]], route, x1, ffn_post_gain, g_f, seq=S, tm=tm_proj)
    return out.reshape(B, S, D)


def kernel(x, c, rel_bias, w_mod, b_mod, attn_pre_gain, attn_post_gain, w_in, shift_mu, w_decay_up, decay_bias,
           w_aicl_up, aicl_bias, w_gate_up, k_k, k_a, r_k, gn_w, gn_b, lam_q1, lam_k1, lam_q2, lam_k2, subln_w,
           w_out, ffn_pre_gain, ffn_post_gain, w_coarse, b_coarse, w_fine, b_fine, w_exp_gate, w_exp_up,
           w_exp_down):
    depth = w_mod.shape[0]
    S = x.shape[1]
    tm_proj = min(256, S)
    tq = min(512, S)
    tm_moe = 512 if S >= 4096 else 128
    for l in range(depth):
        x = _layer(x, c, rel_bias, w_mod[l], b_mod[l], attn_pre_gain[l], attn_post_gain[l], w_in[l], shift_mu[l],
                   w_decay_up[l], decay_bias[l], w_aicl_up[l], aicl_bias[l], w_gate_up[l], k_k[l], k_a[l], r_k[l],
                   gn_w[l], gn_b[l], lam_q1[l], lam_k1[l], lam_q2[l], lam_k2[l], subln_w[l], w_out[l],
                   ffn_pre_gain[l], ffn_post_gain[l], w_coarse[l], b_coarse[l], w_fine[l], b_fine[l],
                   w_exp_gate[l], w_exp_up[l], w_exp_down[l], layer_index=l, tm_proj=tm_proj, tq=tq,
                   rsub_attn=min(256, tq), tm_moe=tm_moe)
    return x
```

```python
import functools
import math

import jax
import jax.numpy as jnp
from jax import lax
from jax.experimental import pallas as pl
from jax.experimental.pallas import tpu as pltpu

F32 = jnp.float32
BF16 = jnp.bfloat16

RWKV_HEAD_DIM = 64
DECAY_RANK = 64
AICL_RANK = 64
GATE_RANK = 128
RWKV_GN_EPS = 64e-5
DIFF_HEADS = 8
DIFF_HEAD_DIM = 64
N_BUCKETS = 32
MAX_DISTANCE = 128
SUBLN_EPS = 1e-5
N_GROUPS = 8
EXPERTS_PER_GROUP = 8
N_EXPERTS = N_GROUPS * EXPERTS_PER_GROUP
NORM_EPS = 1e-6
MASK_VALUE = -1e30
LOG2E = math.log2(math.e)

LANES = 128
RWKV_CHUNK = 64
VMEM_LIMIT = 56 * 1024 * 1024
GATHER_ROWS = 1024
COMBINE_ROWS = 512


def _cparams(sem):
    return pltpu.CompilerParams(dimension_semantics=sem, vmem_limit_bytes=VMEM_LIMIT)


def _dot(a, b):
    return jnp.dot(a, b, preferred_element_type=F32)


def _dot_nt(a, b):
    return lax.dot_general(a, b, (((1,), (1,)), ((), ())), preferred_element_type=F32)


def _dot_tn(a, b):
    return lax.dot_general(a, b, (((0,), (0,)), ((), ())), preferred_element_type=F32)


def _mod_kernel(c_ref, w_ref, b_ref, o_ref):
    c = c_ref[...]
    s = c * jax.nn.sigmoid(c)
    o_ref[...] = _dot(s, w_ref[...]) + b_ref[...]


def _mod(c, w_mod, b_mod, tn=1024):
    B, D = c.shape
    N = w_mod.shape[1]
    rows = 8
    c8 = jnp.zeros((rows, D), F32).at[:B].set(c)
    out = pl.pallas_call(
        _mod_kernel,
        grid=(N // tn,),
        in_specs=[pl.BlockSpec((rows, D), lambda j: (0, 0)),
                  pl.BlockSpec((D, tn), lambda j: (0, j)),
                  pl.BlockSpec((1, tn), lambda j: (0, j))],
        out_specs=pl.BlockSpec((rows, tn), lambda j: (0, j)),
        out_shape=jax.ShapeDtypeStruct((rows, N), F32),
        compiler_params=_cparams(("arbitrary",)),
        name="mod",
    )(c8, w_mod, b_mod.reshape(1, N))
    return out[:B]


def _inproj_kernel(x_ref, gain_ref, sc_ref, sh_ref, w_ref, mu_ref, o_ref, carry_ref, *, shift, tiles_per_batch):
    i = pl.program_id(0)
    x = x_ref[...]
    ms = jnp.mean(x * x, axis=-1, keepdims=True)
    h = x * lax.rsqrt(ms + NORM_EPS) * gain_ref[...]
    h = h * (1.0 + sc_ref[0]) + sh_ref[0]
    p = _dot(h.astype(BF16), w_ref[...])
    if shift:
        tm = p.shape[0]

        @pl.when(i % tiles_per_batch == 0)
        def _():
            carry_ref[...] = jnp.zeros_like(carry_ref)

        prev = pltpu.roll(p, 1, axis=0)
        row = lax.broadcasted_iota(jnp.int32, p.shape, 0)
        prev = jnp.where(row == 0, carry_ref[0:1, :], prev)
        carry_ref[0:1, :] = p[tm - 1:tm, :]
        p = p + (prev - p) * mu_ref[...]
    o_ref[...] = p.astype(o_ref.dtype)


def _inproj(x2, gain, sc, sh, w_bf16, mu, *, shift, out_dtype, seq, tm):
    T, D = x2.shape
    N = w_bf16.shape[1]
    B = T // seq
    tiles_per_batch = seq // tm
    kern = functools.partial(_inproj_kernel, shift=shift, tiles_per_batch=tiles_per_batch)
    return pl.pallas_call(
        kern,
        grid=(T // tm,),
        in_specs=[pl.BlockSpec((tm, D), lambda i: (i, 0)),
                  pl.BlockSpec((1, D), lambda i: (0, 0)),
                  pl.BlockSpec((1, 1, D), lambda i: (i // tiles_per_batch, 0, 0)),
                  pl.BlockSpec((1, 1, D), lambda i: (i // tiles_per_batch, 0, 0)),
                  pl.BlockSpec((D, N), lambda i: (0, 0)),
                  pl.BlockSpec((1, N), lambda i: (0, 0))],
        out_specs=pl.BlockSpec((tm, N), lambda i: (i, 0)),
        out_shape=jax.ShapeDtypeStruct((T, N), out_dtype),
        scratch_shapes=[pltpu.VMEM((8, N), F32)],
        compiler_params=_cparams(("arbitrary",)),
        name="inproj_shift" if shift else "inproj",
    )(x2, gain.reshape(1, D), sc.reshape(B, 1, D), sh.reshape(B, 1, D), w_bf16, mu.reshape(1, N))


def _rwkv_kernel(p_ref, wdu_ref, db_ref, wau_ref, ab_ref, wgu_ref, kk_ref, ka_ref, rk_ref,
                 gnw_ref, gnb_ref, o_ref, s_ref, *, width):
    L = RWKV_CHUNK
    W = width
    n_pairs = W // LANES
    c = pl.program_id(1)

    @pl.when(c == 0)
    def _():
        s_ref[...] = jnp.zeros_like(s_ref)

    r = p_ref[:, 0:W]
    k = p_ref[:, W:2 * W]
    v = p_ref[:, 2 * W:3 * W]
    xw = p_ref[:, 3 * W:3 * W + DECAY_RANK]
    xa = p_ref[:, 3 * W + DECAY_RANK:3 * W + DECAY_RANK + AICL_RANK]
    xg = p_ref[:, 3 * W + DECAY_RANK + AICL_RANK:3 * W + DECAY_RANK + AICL_RANK + GATE_RANK]

    d = db_ref[...] + _dot(jnp.tanh(xw).astype(BF16), wdu_ref[...].astype(BF16))
    lw = -math.exp(-0.5) * jax.nn.sigmoid(d)
    a = jax.nn.sigmoid(ab_ref[...] + _dot(xa.astype(BF16), wau_ref[...].astype(BF16)))
    g = _dot(jax.nn.sigmoid(xg).astype(BF16), wgu_ref[...].astype(BF16))
    kkf = k * kk_ref[...]
    kmod = k * (1.0 + (a - 1.0) * ka_ref[...])

    ri = lax.broadcasted_iota(jnp.int32, (L, L), 0)
    ci = lax.broadcasted_iota(jnp.int32, (L, L), 1)
    tri = (ri >= ci).astype(BF16)
    lw_hi = lw.astype(BF16)
    rem = lw - lw_hi.astype(F32)
    lw_mid = rem.astype(BF16)
    lw_lo = (rem - lw_mid.astype(F32)).astype(BF16)
    cin = _dot(tri, lw_hi) + _dot(tri, lw_mid) + _dot(tri, lw_lo)
    cex = cin - lw
    c_last = cin[L - 1:L, :]
    e_in = jnp.exp(cin)
    e_ni = jnp.exp(-cin)
    e_ex = jnp.exp(cex)
    e_l = jnp.exp(c_last - cin)
    g_last = jnp.exp(c_last)

    lane = lax.broadcasted_iota(jnp.int32, (L, LANES), 1)
    lo = lane < RWKV_HEAD_DIM
    i2 = lax.broadcasted_iota(jnp.int32, (2 * L, LANES), 0)
    j2 = lax.broadcasted_iota(jnp.int32, (2 * L, LANES), 1)
    same_blk = (i2 // L) == (j2 // RWKV_HEAD_DIM)
    mask_strict = same_blk & (i2 > j2)
    mask_incl = same_blk & (i2 >= j2)
    eye = (i2 == j2).astype(F32)

    def stack(t):
        return jnp.concatenate([jnp.where(lo, t, 0.0), jnp.where(lo, 0.0, t)], axis=0)

    def dup(t):
        return jnp.concatenate([t, t], axis=0)

    def headsum(t):
        s_lo = jnp.sum(jnp.where(lo, t, 0.0), axis=-1, keepdims=True)
        s_hi = jnp.sum(jnp.where(lo, 0.0, t), axis=-1, keepdims=True)
        return jnp.where(lo, s_lo, s_hi)

    inv_n = 1.0 / RWKV_HEAD_DIM
    pairs = range(n_pairs)
    sls = [slice(hp * LANES, (hp + 1) * LANES) for hp in pairs]
    lh, rh, v_st, bk = [], [], [], []
    for sl in sls:
        kk_p = kkf[:, sl]
        nrm = jnp.maximum(jnp.sqrt(headsum(kk_p * kk_p)), 1e-12)
        kn = kk_p / nrm
        bp = kn * a[:, sl]
        ag = stack(-kn * e_ex[:, sl])
        rg = stack(r[:, sl] * e_in[:, sl])
        bd = dup(bp * e_ni[:, sl])
        kd = dup(kmod[:, sl] * e_ni[:, sl])
        bl = stack(bp * e_l[:, sl])
        kl = stack(kmod[:, sl] * e_l[:, sl])
        v_st.append(stack(v[:, sl]).astype(BF16))
        lh.append(jnp.concatenate([ag, rg], axis=0).astype(BF16))
        rh.append(jnp.concatenate([bd, kd], axis=0).astype(BF16))
        bk.append(jnp.concatenate([bl, kl], axis=0).astype(BF16))
    a4 = [_dot_nt(lh[i], rh[i]) for i in pairs]
    a_ab = [jnp.where(mask_strict, a4[i][0:2 * L, 0:2 * L], 0.0) for i in pairs]
    a_ak = [jnp.where(mask_strict, a4[i][0:2 * L, 2 * L:4 * L], 0.0).astype(BF16) for i in pairs]
    a_rb = [jnp.where(mask_incl, a4[i][2 * L:4 * L, 0:2 * L], 0.0).astype(BF16) for i in pairs]
    a_rk = [jnp.where(mask_incl, a4[i][2 * L:4 * L, 2 * L:4 * L], 0.0).astype(BF16) for i in pairs]

    pw = a_ab
    tinv = [eye + a_ab[i] for i in pairs]
    for _ in range(int(math.log2(L)) - 1):
        pb = [pw[i].astype(BF16) for i in pairs]
        pw = [_dot(pb[i], pb[i]) for i in pairs]
        tinv = [tinv[i] + _dot(tinv[i].astype(BF16), pw[i].astype(BF16)) for i in pairs]

    s0 = [s_ref[i] for i in pairs]
    xs = [_dot_nt(lh[i], s0[i].astype(BF16)) for i in pairs]
    akv = [_dot(a_ak[i], v_st[i]) for i in pairs]
    u_b = [_dot(tinv[i].astype(BF16), (xs[i][0:2 * L] + akv[i]).astype(BF16)).astype(BF16) for i in pairs]
    y = [xs[i][2 * L:4 * L] + _dot(a_rb[i], u_b[i]) + _dot(a_rk[i], v_st[i]) for i in pairs]
    for i in pairs:
        uv = jnp.concatenate([u_b[i], v_st[i]], axis=0)
        s_ref[i] = s0[i] * g_last[:, sls[i]] + _dot_tn(uv, bk[i])

    for i, sl in enumerate(sls):
        mu = jnp.sum(y[i], axis=-1, keepdims=True) * inv_n
        dlt = jnp.where(same_blk, y[i] - mu, 0.0)
        var = jnp.sum(dlt * dlt, axis=-1, keepdims=True) * inv_n
        yn = dlt * lax.rsqrt(var + RWKV_GN_EPS)
        y_p = (yn[0:L] + yn[L:2 * L]) * gnw_ref[:, sl] + gnb_ref[:, sl]
        bonus = headsum(r[:, sl] * kmod[:, sl] * rk_ref[:, sl]) * v[:, sl]
        o_ref[:, sl] = ((y_p + bonus) * g[:, sl]).astype(o_ref.dtype)


def _rwkv(p_rw, w_decay_up, decay_bias, w_aicl_up, aicl_bias, w_gate_up, k_k, k_a, r_k, gn_w, gn_b, *, seq):
    T, C = p_rw.shape
    W = w_decay_up.shape[1]
    B = T // seq
    L = RWKV_CHUNK
    nC = seq // L
    row = lambda t: t.reshape(1, W)
    full = lambda shape: pl.BlockSpec(shape, lambda b, c: (0,) * len(shape))
    return pl.pallas_call(
        functools.partial(_rwkv_kernel, width=W),
        grid=(B, nC),
        in_specs=[pl.BlockSpec((L, C), lambda b, c: (b * nC + c, 0)),
                  full((DECAY_RANK, W)), full((1, W)), full((AICL_RANK, W)), full((1, W)),
                  full((GATE_RANK, W)), full((1, W)), full((1, W)), full((1, W)), full((1, W)), full((1, W))],
        out_specs=pl.BlockSpec((L, W), lambda b, c: (b * nC + c, 0)),
        out_shape=jax.ShapeDtypeStruct((T, W), BF16),
        scratch_shapes=[pltpu.VMEM((W // LANES, LANES, LANES), F32)],
        compiler_params=_cparams(("arbitrary", "arbitrary")),
        name="rwkv",
    )(p_rw, w_decay_up, row(decay_bias), w_aicl_up, row(aicl_bias), w_gate_up, row(k_k), row(k_a),
      row(r_k), row(gn_w), row(gn_b))


def _t5_bucket(rel):
    n = jnp.maximum(rel, 0)
    max_exact = N_BUCKETS // 2
    nf = jnp.maximum(n, 1).astype(F32)
    large = max_exact + (jnp.log(nf / max_exact) / math.log(MAX_DISTANCE / max_exact)
                         * (N_BUCKETS - max_exact)).astype(jnp.int32)
    large = jnp.minimum(large, N_BUCKETS - 1)
    return jnp.where(n < max_exact, n, large)


def _attn_kernel(q_ref, k_ref, v_ref, bd_ref, bn_ref, lq1_ref, lk1_ref, lq2_ref, lk2_ref,
                 sw_ref, o_ref, qs_ref, m_ref, l_ref, acc_ref, *, tq, rsub, lambda_init):
    qi = pl.program_id(2)
    tk = tq

    q = q_ref[...]
    lane = lax.broadcasted_iota(jnp.int32, q.shape, 1)
    zero = jnp.zeros_like(q)
    qs_ref[0:tq, :] = jnp.where(lane < DIFF_HEAD_DIM, q, zero)
    qs_ref[tq:2 * tq, :] = jnp.where(lane < DIFF_HEAD_DIM, zero, q)
    m_ref[...] = jnp.full_like(m_ref, MASK_VALUE)
    l_ref[...] = jnp.zeros_like(l_ref)
    acc_ref[...] = jnp.zeros_like(acc_ref)

    def tile(kj, bias_ref, width=tk):
        koff = pl.multiple_of(kj * tk, tk)
        k = k_ref[pl.ds(koff, width), :]
        v = v_ref[pl.ds(koff, width), :]
        subs = [slice(sb * rsub, (sb + 1) * rsub) for sb in range(2 * tq // rsub)]
        m_prev = [m_ref[rows, :] for rows in subs]
        l_prev = [l_ref[rows, :] for rows in subs]
        acc_prev = [acc_ref[rows, :] for rows in subs]
        s_all = [_dot_nt(qs_ref[rows, :], k) for rows in subs]
        m_out, l_out, alphas, ps = [], [], [], []
        for sb, s in enumerate(s_all):
            if bias_ref is not None:
                b0 = (sb * rsub) % tq
                s = s + bias_ref[0, b0:b0 + rsub, :]
            m_new = jnp.maximum(m_prev[sb], jnp.max(s, axis=-1, keepdims=True))
            alpha = jnp.exp2(m_prev[sb] - m_new)
            p = jnp.exp2(s - jnp.concatenate([m_new] * (width // LANES), axis=1))
            m_out.append(m_new)
            l_out.append(alpha * l_prev[sb] + jnp.sum(p, axis=-1, keepdims=True))
            alphas.append(alpha)
            ps.append(p.astype(BF16))
        pv = [_dot(p, v) for p in ps]
        for sb, rows in enumerate(subs):
            m_ref[rows, :] = m_out[sb]
            l_ref[rows, :] = l_out[sb]
            acc_ref[rows, :] = alphas[sb] * acc_prev[sb] + pv[sb]

    n_far = jnp.maximum(qi - 1, 0)

    def far_pair(j, carry):
        tile(2 * j, None, width=2 * tk)
        return carry

    lax.fori_loop(0, n_far // 2, far_pair, 0)

    @pl.when(n_far % 2 == 1)
    def _():
        tile(n_far - 1, None)

    @pl.when(qi >= 1)
    def _():
        tile(qi - 1, bn_ref)

    tile(qi, bd_ref)
    l = l_ref[...]
    acc = acc_ref[...]
    o1 = acc[0:tq] / l[0:tq]
    o2 = acc[tq:2 * tq] / l[tq:2 * tq]
    lam = (jnp.exp(jnp.sum(lq1_ref[...] * lk1_ref[...], axis=-1, keepdims=True))
           - jnp.exp(jnp.sum(lq2_ref[...] * lk2_ref[...], axis=-1, keepdims=True)) + lambda_init)
    o = o1 - lam * o2
    o = o * lax.rsqrt(jnp.mean(o * o, axis=-1, keepdims=True) + SUBLN_EPS)
    o = o * sw_ref[...] * (1.0 - lambda_init)
    o_ref[...] = o.astype(o_ref.dtype)


def _diff_attention(p_da, rel_bias, lam_q1, lam_k1, lam_q2, lam_k2, subln_w, lambda_init, *, seq, tq, rsub):
    T, C = p_da.shape
    H = DIFF_HEADS
    B = T // seq
    nq = seq // tq
    hd2 = 2 * DIFF_HEAD_DIM
    far = rel_bias[N_BUCKETS - 1]
    ii = jnp.arange(tq)[:, None]
    jj = jnp.arange(tq)[None, :]
    rel_d = ii - jj
    rel_n = tq + ii - jj
    tbl = lambda rel: jnp.moveaxis(rel_bias[_t5_bucket(rel)] - far, -1, 0) * LOG2E
    bias_d = jnp.where(rel_d >= 0, tbl(rel_d), MASK_VALUE).astype(F32)
    bias_n = tbl(rel_n).astype(F32)
    vec = lambda t: t.reshape(1, -1)
    small = lambda n: pl.BlockSpec((1, n), lambda b, h, i: (0, 0))
    return pl.pallas_call(
        functools.partial(_attn_kernel, tq=tq, rsub=rsub, lambda_init=lambda_init),
        grid=(B, H, nq),
        in_specs=[pl.BlockSpec((tq, hd2), lambda b, h, i: (b * nq + i, h)),
                  pl.BlockSpec((seq, hd2), lambda b, h, i: (b, H + h)),
                  pl.BlockSpec((seq, hd2), lambda b, h, i: (b, 2 * H + h)),
                  pl.BlockSpec((1, tq, tq), lambda b, h, i: (h, 0, 0)),
                  pl.BlockSpec((1, tq, tq), lambda b, h, i: (h, 0, 0)),
                  small(DIFF_HEAD_DIM), small(DIFF_HEAD_DIM), small(DIFF_HEAD_DIM), small(DIFF_HEAD_DIM),
                  small(hd2)],
        out_specs=pl.BlockSpec((tq, hd2), lambda b, h, i: (b * nq + i, h)),
        out_shape=jax.ShapeDtypeStruct((T, H * hd2), BF16),
        scratch_shapes=[pltpu.VMEM((2 * tq, hd2), BF16), pltpu.VMEM((2 * tq, LANES), F32),
                        pltpu.VMEM((2 * tq, LANES), F32), pltpu.VMEM((2 * tq, hd2), F32)],
        compiler_params=_cparams(("arbitrary", "arbitrary", "arbitrary")),
        name="diff_attn",
    )(p_da, p_da, p_da, bias_d, bias_n, vec(lam_q1), vec(lam_k1), vec(lam_q2), vec(lam_k2), vec(subln_w))


def _outproj_kernel(orw_ref, oda_ref, x_ref, w1_ref, w2_ref, pg_ref, ga_ref, fg_ref, scf_ref, shf_ref,
                    wr_ref, br_ref, x1_ref, h2_ref, route_ref):
    mix = _dot(orw_ref[...], w1_ref[...]) + _dot(oda_ref[...], w2_ref[...])
    y = mix * lax.rsqrt(jnp.mean(mix * mix, axis=-1, keepdims=True) + NORM_EPS) * pg_ref[...]
    x1 = x_ref[...] + ga_ref[0] * y
    x1_ref[...] = x1
    h2 = x1 * lax.rsqrt(jnp.mean(x1 * x1, axis=-1, keepdims=True) + NORM_EPS) * fg_ref[...]
    h2 = h2 * (1.0 + scf_ref[0]) + shf_ref[0]
    h2_ref[...] = h2.astype(h2_ref.dtype)

    logits = jnp.dot(h2, wr_ref[...], preferred_element_type=F32, precision=lax.Precision.HIGHEST) + br_ref[...]
    lane = lax.broadcasted_iota(jnp.int32, logits.shape, 1).astype(F32)
    big = float(LANES)
    neg = -jnp.inf
    cl = jnp.where(lane < N_GROUPS, logits, neg)
    cmax = jnp.max(cl, axis=-1, keepdims=True)
    grp = jnp.min(jnp.where(cl == cmax, lane, big), axis=-1, keepdims=True)
    grp_p = 1.0 / jnp.sum(jnp.exp(cl - cmax), axis=-1, keepdims=True)
    f_lo = N_GROUPS + EXPERTS_PER_GROUP * grp
    fl = jnp.where((lane >= f_lo) & (lane < f_lo + EXPERTS_PER_GROUP), logits, neg)
    v1 = jnp.max(fl, axis=-1, keepdims=True)
    i1 = jnp.min(jnp.where(fl == v1, lane, big), axis=-1, keepdims=True)
    fl2 = jnp.where(lane == i1, neg, fl)
    v2 = jnp.max(fl2, axis=-1, keepdims=True)
    i2 = jnp.min(jnp.where(fl2 == v2, lane, big), axis=-1, keepdims=True)
    e21 = jnp.exp(v2 - v1)
    w1 = grp_p / (1.0 + e21)
    w2 = w1 * e21
    route = jnp.where(lane == 0, i1 - N_GROUPS,
                      jnp.where(lane == 1, i2 - N_GROUPS,
                                jnp.where(lane == 2, w1, jnp.where(lane == 3, w2, 0.0))))
    route_ref[...] = route


def _outproj(o_rw, o_da, x2, w_out_bf16, post_gain, g_a, ffn_gain, sc_f, sh_f, w_route, b_route, *, seq, tm):
    T, D = x2.shape
    W1 = o_rw.shape[1]
    W2 = o_da.shape[1]
    B = T // seq
    tpb = seq // tm
    rowD = lambda: pl.BlockSpec((1, D), lambda i: (0, 0))
    perb = lambda: pl.BlockSpec((1, 1, D), lambda i: (i // tpb, 0, 0))
    r3 = lambda t: t.reshape(B, 1, D)
    return pl.pallas_call(
        _outproj_kernel,
        grid=(T // tm,),
        in_specs=[pl.BlockSpec((tm, W1), lambda i: (i, 0)),
                  pl.BlockSpec((tm, W2), lambda i: (i, 0)),
                  pl.BlockSpec((tm, D), lambda i: (i, 0)),
                  pl.BlockSpec((W1, D), lambda i: (0, 0)),
                  pl.BlockSpec((W2, D), lambda i: (1, 0)),
                  rowD(), perb(), rowD(), perb(), perb(),
                  pl.BlockSpec((D, LANES), lambda i: (0, 0)),
                  pl.BlockSpec((1, LANES), lambda i: (0, 0))],
        out_specs=[pl.BlockSpec((tm, D), lambda i: (i, 0)),
                   pl.BlockSpec((tm, D), lambda i: (i, 0)),
                   pl.BlockSpec((tm, LANES), lambda i: (i, 0))],
        out_shape=[jax.ShapeDtypeStruct((T, D), F32),
                   jax.ShapeDtypeStruct((T, D), F32),
                   jax.ShapeDtypeStruct((T, LANES), F32)],
        compiler_params=_cparams(("arbitrary",)),
        name="outproj_router",
    )(o_rw, o_da, x2, w_out_bf16, w_out_bf16, post_gain.reshape(1, D), r3(g_a), ffn_gain.reshape(1, D),
      r3(sc_f), r3(sh_f), w_route, b_route)


def _new_expert(te_ref, i):
    return (i == 0) | (te_ref[i] != te_ref[jnp.maximum(i - 1, 0)])


def _expert_up_kernel(te_ref, nu_ref, x_ref, wg_ref, wu_ref, o_ref, wgb_ref, wub_ref):
    i = pl.program_id(1)

    @pl.when(i < nu_ref[0])
    def _():
        @pl.when(_new_expert(te_ref, i))
        def _():
            wgb_ref[...] = wg_ref[0].astype(BF16)
            wub_ref[...] = wu_ref[0].astype(BF16)

        x = x_ref[...]
        gt = _dot(x, wgb_ref[...])
        up = _dot(x, wub_ref[...])
        o_ref[...] = (gt * jax.nn.sigmoid(gt) * up).astype(o_ref.dtype)

    @pl.when(i >= nu_ref[0])
    def _():
        o_ref[...] = jnp.zeros_like(o_ref)


def _expert_down_kernel(te_ref, nu_ref, a_ref, wd_ref, o_ref, wdb_ref):
    i = pl.program_id(1)

    @pl.when(i < nu_ref[0])
    def _():
        @pl.when(_new_expert(te_ref, i))
        def _():
            wdb_ref[...] = wd_ref[0].astype(BF16)

        o_ref[...] = _dot(a_ref[...], wdb_ref[...]).astype(o_ref.dtype)

    @pl.when(i >= nu_ref[0])
    def _():
        o_ref[...] = jnp.zeros_like(o_ref)


def _experts(xs, tile_expert, n_used, wg, wu, wd, *, tm, n_split=2):
    P, D = xs.shape
    F = wg.shape[2]
    n_tiles = P // tm
    fh = F // n_split
    dh = D // n_split
    row = lambda j, i, te, nu: jnp.minimum(i, nu[0] - 1)
    act = pl.pallas_call(
        _expert_up_kernel,
        grid_spec=pltpu.PrefetchScalarGridSpec(
            num_scalar_prefetch=2,
            grid=(n_split, n_tiles),
            in_specs=[pl.BlockSpec((tm, D), lambda j, i, te, nu: (row(j, i, te, nu), 0)),
                      pl.BlockSpec((1, D, fh), lambda j, i, te, nu: (te[i], 0, j)),
                      pl.BlockSpec((1, D, fh), lambda j, i, te, nu: (te[i], 0, j))],
            out_specs=pl.BlockSpec((tm, fh), lambda j, i, te, nu: (i, j)),
            scratch_shapes=[pltpu.VMEM((D, fh), BF16), pltpu.VMEM((D, fh), BF16)],
        ),
        out_shape=jax.ShapeDtypeStruct((P, F), BF16),
        compiler_params=_cparams(("arbitrary", "arbitrary")),
        name="experts_up",
    )(tile_expert, n_used, xs, wg, wu)
    return pl.pallas_call(
        _expert_down_kernel,
        grid_spec=pltpu.PrefetchScalarGridSpec(
            num_scalar_prefetch=2,
            grid=(n_split, n_tiles),
            in_specs=[pl.BlockSpec((tm, F), lambda j, i, te, nu: (row(j, i, te, nu), 0)),
                      pl.BlockSpec((1, F, dh), lambda j, i, te, nu: (te[i], 0, j))],
            out_specs=pl.BlockSpec((tm, dh), lambda j, i, te, nu: (i, j)),
            scratch_shapes=[pltpu.VMEM((F, dh), BF16)],
        ),
        out_shape=jax.ShapeDtypeStruct((P, D), F32),
        compiler_params=_cparams(("arbitrary", "arbitrary")),
        name="experts_down",
    )(tile_expert, n_used, act, wd)


def _row_copy(src_ref, row, dst_ref, slot, sem):
    return pltpu.make_async_copy(src_ref.at[row], dst_ref.at[slot], sem)


def _wait_rows(src_ref, dst_ref, sem):
    pltpu.make_async_copy(src_ref.at[pl.ds(0, dst_ref.shape[0])], dst_ref, sem).wait()


def _gather_kernel(ns_ref, idx_ref, src_ref, o_ref, buf_ref, sem):
    i = pl.program_id(0)

    @pl.when(i < ns_ref[0])
    def _():
        def issue(r, carry):
            _row_copy(src_ref, idx_ref[r], buf_ref, r, sem).start()
            return carry

        lax.fori_loop(0, buf_ref.shape[0], issue, 0, unroll=8)
        _wait_rows(src_ref, buf_ref, sem)
        o_ref[...] = buf_ref[...].astype(o_ref.dtype)

    @pl.when(i >= ns_ref[0])
    def _():
        o_ref[...] = jnp.zeros_like(o_ref)


def _gather_rows(src, idx, n_steps_used, *, tg, out_dtype):
    P = idx.shape[0]
    D = src.shape[1]
    return pl.pallas_call(
        _gather_kernel,
        grid_spec=pltpu.PrefetchScalarGridSpec(
            num_scalar_prefetch=1,
            grid=(P // tg,),
            in_specs=[pl.BlockSpec((tg,), lambda i, ns: (i,), memory_space=pltpu.SMEM),
                      pl.BlockSpec(memory_space=pl.ANY)],
            out_specs=pl.BlockSpec((tg, D), lambda i, ns: (i, 0)),
            scratch_shapes=[pltpu.VMEM((tg, D), src.dtype), pltpu.SemaphoreType.DMA(())],
        ),
        out_shape=jax.ShapeDtypeStruct((P, D), out_dtype),
        compiler_params=pltpu.CompilerParams(dimension_semantics=("arbitrary",), vmem_limit_bytes=VMEM_LIMIT,
                                             disable_bounds_checks=True),
        name="dispatch_gather",
    )(n_steps_used, idx, src)


def _combine_kernel(pos_ref, ys_ref, route_ref, x1_ref, pg_ref, gf_ref, o_ref, y0_ref, y1_ref, sem0, sem1):
    tm = x1_ref.shape[0]

    def issue(t, carry):
        _row_copy(ys_ref, pos_ref[2 * t], y0_ref, t, sem0).start()
        _row_copy(ys_ref, pos_ref[2 * t + 1], y1_ref, t, sem1).start()
        return carry

    lax.fori_loop(0, tm, issue, 0, unroll=8)
    route = route_ref[...]
    w0 = route[:, 2:3]
    w1 = route[:, 3:4]
    _wait_rows(ys_ref, y0_ref, sem0)
    _wait_rows(ys_ref, y1_ref, sem1)
    y = y0_ref[...] * w0 + y1_ref[...] * w1
    yn = y * lax.rsqrt(jnp.mean(y * y, axis=-1, keepdims=True) + NORM_EPS) * pg_ref[...]
    o_ref[...] = x1_ref[...] + gf_ref[0] * yn


def _combine(ys, pos, route, x1, post_gain, g_f, *, seq, tm):
    T, D = x1.shape
    B = T // seq
    tpb = seq // tm
    return pl.pallas_call(
        _combine_kernel,
        grid=(T // tm,),
        in_specs=[pl.BlockSpec((2 * tm,), lambda i: (i,), memory_space=pltpu.SMEM),
                  pl.BlockSpec(memory_space=pl.ANY),
                  pl.BlockSpec((tm, LANES), lambda i: (i, 0)),
                  pl.BlockSpec((tm, D), lambda i: (i, 0)),
                  pl.BlockSpec((1, D), lambda i: (0, 0)),
                  pl.BlockSpec((1, 1, D), lambda i: (i // tpb, 0, 0))],
        out_specs=pl.BlockSpec((tm, D), lambda i: (i, 0)),
        out_shape=jax.ShapeDtypeStruct((T, D), F32),
        scratch_shapes=[pltpu.VMEM((tm, D), ys.dtype), pltpu.VMEM((tm, D), ys.dtype),
                        pltpu.SemaphoreType.DMA(()), pltpu.SemaphoreType.DMA(())],
        compiler_params=pltpu.CompilerParams(dimension_semantics=("arbitrary",), vmem_limit_bytes=VMEM_LIMIT,
                                             disable_bounds_checks=True),
        name="combine",
    )(pos, ys, route, x1, post_gain.reshape(1, D), g_f.reshape(B, 1, D))


def _dispatch_plan(expert_flat, tm, n_tiles):
    n = expert_flat.shape[0]
    onehot = (expert_flat[:, None] == jnp.arange(N_EXPERTS, dtype=jnp.int32)[None, :]).astype(jnp.int32)
    csum = jnp.cumsum(onehot, axis=0)
    counts = csum[-1]
    rank = jnp.sum((csum - onehot) * onehot, axis=1)
    tiles_e = (counts + tm - 1) // tm
    tile_end = jnp.cumsum(tiles_e)
    tile_start = tile_end - tiles_e
    pos = tile_start[expert_flat] * tm + rank
    n_used = tile_end[-1]
    tile_ids = jnp.minimum(jnp.arange(n_tiles, dtype=jnp.int32), n_used - 1)
    tile_expert = jnp.minimum(jnp.searchsorted(tile_end, tile_ids, side="right"), N_EXPERTS - 1).astype(jnp.int32)
    row_token = jnp.zeros((n_tiles * tm,), jnp.int32).at[pos].set(jnp.arange(n, dtype=jnp.int32) // 2)
    return pos, row_token, tile_expert, n_used.astype(jnp.int32).reshape(1)


def _layer(x, c, rel_bias, w_mod, b_mod, attn_pre_gain, attn_post_gain, w_in, shift_mu, w_decay_up, decay_bias,
           w_aicl_up, aicl_bias, w_gate_up, k_k, k_a, r_k, gn_w, gn_b, lam_q1, lam_k1, lam_q2, lam_k2, subln_w,
           w_out, ffn_pre_gain, ffn_post_gain, w_coarse, b_coarse, w_fine, b_fine, w_exp_gate, w_exp_up,
           w_exp_down, *, layer_index, tm_proj, tq, rsub_attn, tm_moe):
    B, S, D = x.shape
    T = B * S
    W = w_decay_up.shape[1]
    rw_cols = shift_mu.shape[0]
    lambda_init = 0.8 - 0.6 * math.exp(-0.3 * layer_index)
    x2 = x.reshape(T, D)

    mod = _mod(c, w_mod, b_mod)
    sh_a, sc_a, g_a, sh_f, sc_f, g_f = jnp.split(mod, 6, axis=-1)

    q_cols = DIFF_HEADS * 2 * DIFF_HEAD_DIM
    q_scale = DIFF_HEAD_DIM ** -0.5 * LOG2E
    w_rw_b = w_in[:, :rw_cols].astype(BF16)
    w_da_b = jnp.concatenate([w_in[:, rw_cols:rw_cols + q_cols] * q_scale, w_in[:, rw_cols + q_cols:]],
                             axis=1).astype(BF16)
    p_rw = _inproj(x2, attn_pre_gain, sc_a, sh_a, w_rw_b, shift_mu,
                   shift=True, out_dtype=F32, seq=S, tm=tm_proj)
    p_da = _inproj(x2, attn_pre_gain, sc_a, sh_a, w_da_b, jnp.zeros((w_in.shape[1] - rw_cols,), F32),
                   shift=False, out_dtype=BF16, seq=S, tm=tm_proj)

    o_rw = _rwkv(p_rw, w_decay_up, decay_bias, w_aicl_up, aicl_bias, w_gate_up, k_k, k_a, r_k.reshape(-1),
                 gn_w, gn_b, seq=S)
    o_da = _diff_attention(p_da, rel_bias, lam_q1, lam_k1, lam_q2, lam_k2, subln_w, lambda_init, seq=S, tq=tq,
                           rsub=rsub_attn)

    w_route = jnp.zeros((D, LANES), F32).at[:, :N_GROUPS].set(w_coarse).at[:, N_GROUPS:N_GROUPS + N_EXPERTS].set(w_fine)
    b_route = jnp.zeros((1, LANES), F32).at[0, :N_GROUPS].set(b_coarse).at[0, N_GROUPS:N_GROUPS + N_EXPERTS].set(b_fine)
    x1, h2, route = _outproj(o_rw, o_da, x2, w_out.astype(BF16), attn_post_gain, g_a, ffn_pre_gain, sc_f, sh_f,
                             w_route, b_route, seq=S, tm=tm_proj)

    expert_flat = route[:, 0:2].astype(jnp.int32).reshape(-1)
    n_tiles = (2 * T) // tm_moe + N_EXPERTS
    pos, row_token, tile_expert, n_used = _dispatch_plan(expert_flat, tm_moe, n_tiles)
    gather_steps_used = (n_used * tm_moe + GATHER_ROWS - 1) // GATHER_ROWS
    xs = _gather_rows(h2, row_token, gather_steps_used, tg=GATHER_ROWS, out_dtype=BF16)
    ys = _experts(xs, tile_expert, n_used, w_exp_gate, w_exp_up, w_exp_down, tm=tm_moe)
    out = _combine(ys, pos, route, x1, ffn_post_gain, g_f, seq=S, tm=COMBINE_ROWS)
    return out.reshape(B, S, D)


def kernel(x, c, rel_bias, w_mod, b_mod, attn_pre_gain, attn_post_gain, w_in, shift_mu, w_decay_up, decay_bias,
           w_aicl_up, aicl_bias, w_gate_up, k_k, k_a, r_k, gn_w, gn_b, lam_q1, lam_k1, lam_q2, lam_k2, subln_w,
           w_out, ffn_pre_gain, ffn_post_gain, w_coarse, b_coarse, w_fine, b_fine, w_exp_gate, w_exp_up,
           w_exp_down):
    depth = w_mod.shape[0]
    S = x.shape[1]
    tm_proj = min(256, S)
    tq = min(512, S)
    tm_moe = 512 if S >= 4096 else 128
    for l in range(depth):
        x = _layer(x, c, rel_bias, w_mod[l], b_mod[l], attn_pre_gain[l], attn_post_gain[l], w_in[l], shift_mu[l],
                   w_decay_up[l], decay_bias[l], w_aicl_up[l], aicl_bias[l], w_gate_up[l], k_k[l], k_a[l], r_k[l],
                   gn_w[l], gn_b[l], lam_q1[l], lam_k1[l], lam_q2[l], lam_k2[l], subln_w[l], w_out[l],
                   ffn_pre_gain[l], ffn_post_gain[l], w_coarse[l], b_coarse[l], w_fine[l], b_fine[l],
                   w_exp_gate[l], w_exp_up[l], w_exp_down[l], layer_index=l, tm_proj=tm_proj, tq=tq,
                   rsub_attn=min(256, tq), tm_moe=tm_moe)
    return x
```

```python
import functools
import math

import jax
import jax.numpy as jnp
from jax import lax
from jax.experimental import pallas as pl
from jax.experimental.pallas import tpu as pltpu

F32 = jnp.float32
BF16 = jnp.bfloat16

RWKV_HEAD_DIM = 64
DECAY_RANK = 64
AICL_RANK = 64
GATE_RANK = 128
RWKV_GN_EPS = 64e-5
DIFF_HEADS = 8
DIFF_HEAD_DIM = 64
N_BUCKETS = 32
MAX_DISTANCE = 128
SUBLN_EPS = 1e-5
N_GROUPS = 8
EXPERTS_PER_GROUP = 8
N_EXPERTS = N_GROUPS * EXPERTS_PER_GROUP
NORM_EPS = 1e-6
MASK_VALUE = -1e30
LOG2E = math.log2(math.e)

LANES = 128
RWKV_CHUNK = 64
VMEM_LIMIT = 56 * 1024 * 1024
GATHER_ROWS = 1024
COMBINE_ROWS = 512
EXPERT_SPLIT = 2


def _cparams(sem):
    return pltpu.CompilerParams(dimension_semantics=sem, vmem_limit_bytes=VMEM_LIMIT)


def _dot(a, b):
    return jnp.dot(a, b, preferred_element_type=F32)


def _dot_nt(a, b):
    return lax.dot_general(a, b, (((1,), (1,)), ((), ())), preferred_element_type=F32)


def _dot_tn(a, b):
    return lax.dot_general(a, b, (((0,), (0,)), ((), ())), preferred_element_type=F32)


def _mod_kernel(c_ref, w_ref, b_ref, o_ref):
    c = c_ref[...]
    s = c * jax.nn.sigmoid(c)
    o_ref[...] = _dot(s, w_ref[...]) + b_ref[...]


def _mod(c, w_mod, b_mod, tn=1024):
    B, D = c.shape
    N = w_mod.shape[1]
    rows = 8
    c8 = jnp.zeros((rows, D), F32).at[:B].set(c)
    out = pl.pallas_call(
        _mod_kernel,
        grid=(N // tn,),
        in_specs=[pl.BlockSpec((rows, D), lambda j: (0, 0)),
                  pl.BlockSpec((D, tn), lambda j: (0, j)),
                  pl.BlockSpec((1, tn), lambda j: (0, j))],
        out_specs=pl.BlockSpec((rows, tn), lambda j: (0, j)),
        out_shape=jax.ShapeDtypeStruct((rows, N), F32),
        compiler_params=_cparams(("arbitrary",)),
        name="mod",
    )(c8, w_mod, b_mod.reshape(1, N))
    return out[:B]


def _inproj_kernel(x_ref, gain_ref, sc_ref, sh_ref, w_ref, mu_ref, o_ref, carry_ref, *, shift, tiles_per_batch):
    i = pl.program_id(0)
    x = x_ref[...]
    ms = jnp.mean(x * x, axis=-1, keepdims=True)
    h = x * lax.rsqrt(ms + NORM_EPS) * gain_ref[...]
    h = h * (1.0 + sc_ref[0]) + sh_ref[0]
    p = _dot(h.astype(BF16), w_ref[...])
    if shift:
        tm = p.shape[0]

        @pl.when(i % tiles_per_batch == 0)
        def _():
            carry_ref[...] = jnp.zeros_like(carry_ref)

        prev = pltpu.roll(p, 1, axis=0)
        row = lax.broadcasted_iota(jnp.int32, p.shape, 0)
        prev = jnp.where(row == 0, carry_ref[0:1, :], prev)
        carry_ref[0:1, :] = p[tm - 1:tm, :]
        p = p + (prev - p) * mu_ref[...]
    o_ref[...] = p.astype(o_ref.dtype)


def _inproj(x2, gain, sc, sh, w_bf16, mu, *, shift, out_dtype, seq, tm):
    T, D = x2.shape
    N = w_bf16.shape[1]
    B = T // seq
    tiles_per_batch = seq // tm
    kern = functools.partial(_inproj_kernel, shift=shift, tiles_per_batch=tiles_per_batch)
    return pl.pallas_call(
        kern,
        grid=(T // tm,),
        in_specs=[pl.BlockSpec((tm, D), lambda i: (i, 0)),
                  pl.BlockSpec((1, D), lambda i: (0, 0)),
                  pl.BlockSpec((1, 1, D), lambda i: (i // tiles_per_batch, 0, 0)),
                  pl.BlockSpec((1, 1, D), lambda i: (i // tiles_per_batch, 0, 0)),
                  pl.BlockSpec((D, N), lambda i: (0, 0)),
                  pl.BlockSpec((1, N), lambda i: (0, 0))],
        out_specs=pl.BlockSpec((tm, N), lambda i: (i, 0)),
        out_shape=jax.ShapeDtypeStruct((T, N), out_dtype),
        scratch_shapes=[pltpu.VMEM((8, N), F32)],
        compiler_params=_cparams(("arbitrary",)),
        name="inproj_shift" if shift else "inproj",
    )(x2, gain.reshape(1, D), sc.reshape(B, 1, D), sh.reshape(B, 1, D), w_bf16, mu.reshape(1, N))


def _rwkv_kernel(p_ref, wdu_ref, db_ref, wau_ref, ab_ref, wgu_ref, kk_ref, ka_ref, rk_ref,
                 gnw_ref, gnb_ref, o_ref, s_ref, *, width):
    L = RWKV_CHUNK
    W = width
    n_pairs = W // LANES
    c = pl.program_id(1)

    @pl.when(c == 0)
    def _():
        s_ref[...] = jnp.zeros_like(s_ref)

    r = p_ref[:, 0:W]
    k = p_ref[:, W:2 * W]
    v = p_ref[:, 2 * W:3 * W]
    xw = p_ref[:, 3 * W:3 * W + DECAY_RANK]
    xa = p_ref[:, 3 * W + DECAY_RANK:3 * W + DECAY_RANK + AICL_RANK]
    xg = p_ref[:, 3 * W + DECAY_RANK + AICL_RANK:3 * W + DECAY_RANK + AICL_RANK + GATE_RANK]

    d = db_ref[...] + _dot(jnp.tanh(xw).astype(BF16), wdu_ref[...].astype(BF16))
    lw = -math.exp(-0.5) * jax.nn.sigmoid(d)
    a = jax.nn.sigmoid(ab_ref[...] + _dot(xa.astype(BF16), wau_ref[...].astype(BF16)))
    g = _dot(jax.nn.sigmoid(xg).astype(BF16), wgu_ref[...].astype(BF16))
    kkf = k * kk_ref[...]
    kmod = k * (1.0 + (a - 1.0) * ka_ref[...])

    ri = lax.broadcasted_iota(jnp.int32, (L, L), 0)
    ci = lax.broadcasted_iota(jnp.int32, (L, L), 1)
    tri = (ri >= ci).astype(BF16)
    lw_hi = lw.astype(BF16)
    rem = lw - lw_hi.astype(F32)
    lw_mid = rem.astype(BF16)
    lw_lo = (rem - lw_mid.astype(F32)).astype(BF16)
    cin = _dot(tri, lw_hi) + _dot(tri, lw_mid) + _dot(tri, lw_lo)
    cex = cin - lw
    c_last = cin[L - 1:L, :]
    e_in = jnp.exp(cin)
    e_ni = jnp.exp(-cin)
    e_ex = jnp.exp(cex)
    e_l = jnp.exp(c_last - cin)
    g_last = jnp.exp(c_last)

    lane = lax.broadcasted_iota(jnp.int32, (L, LANES), 1)
    lo = lane < RWKV_HEAD_DIM
    i2 = lax.broadcasted_iota(jnp.int32, (2 * L, LANES), 0)
    j2 = lax.broadcasted_iota(jnp.int32, (2 * L, LANES), 1)
    same_blk = (i2 // L) == (j2 // RWKV_HEAD_DIM)
    mask_strict = same_blk & (i2 > j2)
    mask_incl = same_blk & (i2 >= j2)
    eye = (i2 == j2).astype(F32)

    def stack(t):
        return jnp.concatenate([jnp.where(lo, t, 0.0), jnp.where(lo, 0.0, t)], axis=0)

    def dup(t):
        return jnp.concatenate([t, t], axis=0)

    def headsum(t):
        s_lo = jnp.sum(jnp.where(lo, t, 0.0), axis=-1, keepdims=True)
        s_hi = jnp.sum(jnp.where(lo, 0.0, t), axis=-1, keepdims=True)
        return jnp.where(lo, s_lo, s_hi)

    inv_n = 1.0 / RWKV_HEAD_DIM
    pairs = range(n_pairs)
    sls = [slice(hp * LANES, (hp + 1) * LANES) for hp in pairs]
    lh, rh, v_st, bk = [], [], [], []
    for sl in sls:
        kk_p = kkf[:, sl]
        nrm = jnp.maximum(jnp.sqrt(headsum(kk_p * kk_p)), 1e-12)
        kn = kk_p / nrm
        bp = kn * a[:, sl]
        ag = stack(-kn * e_ex[:, sl])
        rg = stack(r[:, sl] * e_in[:, sl])
        bd = dup(bp * e_ni[:, sl])
        kd = dup(kmod[:, sl] * e_ni[:, sl])
        bl = stack(bp * e_l[:, sl])
        kl = stack(kmod[:, sl] * e_l[:, sl])
        v_st.append(stack(v[:, sl]).astype(BF16))
        lh.append(jnp.concatenate([ag, rg], axis=0).astype(BF16))
        rh.append(jnp.concatenate([bd, kd], axis=0).astype(BF16))
        bk.append(jnp.concatenate([bl, kl], axis=0).astype(BF16))
    a4 = [_dot_nt(lh[i], rh[i]) for i in pairs]
    a_ab = [jnp.where(mask_strict, a4[i][0:2 * L, 0:2 * L], 0.0) for i in pairs]
    a_ak = [jnp.where(mask_strict, a4[i][0:2 * L, 2 * L:4 * L], 0.0).astype(BF16) for i in pairs]
    a_rb = [jnp.where(mask_incl, a4[i][2 * L:4 * L, 0:2 * L], 0.0).astype(BF16) for i in pairs]
    a_rk = [jnp.where(mask_incl, a4[i][2 * L:4 * L, 2 * L:4 * L], 0.0).astype(BF16) for i in pairs]

    pw = a_ab
    tinv = [eye + a_ab[i] for i in pairs]
    for _ in range(int(math.log2(L)) - 1):
        pb = [pw[i].astype(BF16) for i in pairs]
        pw = [_dot(pb[i], pb[i]) for i in pairs]
        tinv = [tinv[i] + _dot(tinv[i].astype(BF16), pw[i].astype(BF16)) for i in pairs]

    s0 = [s_ref[i] for i in pairs]
    xs = [_dot_nt(lh[i], s0[i].astype(BF16)) for i in pairs]
    akv = [_dot(a_ak[i], v_st[i]) for i in pairs]
    u_b = [_dot(tinv[i].astype(BF16), (xs[i][0:2 * L] + akv[i]).astype(BF16)).astype(BF16) for i in pairs]
    y = [xs[i][2 * L:4 * L] + _dot(a_rb[i], u_b[i]) + _dot(a_rk[i], v_st[i]) for i in pairs]
    for i in pairs:
        uv = jnp.concatenate([u_b[i], v_st[i]], axis=0)
        s_ref[i] = s0[i] * g_last[:, sls[i]] + _dot_tn(uv, bk[i])

    for i, sl in enumerate(sls):
        mu = jnp.sum(y[i], axis=-1, keepdims=True) * inv_n
        dlt = jnp.where(same_blk, y[i] - mu, 0.0)
        var = jnp.sum(dlt * dlt, axis=-1, keepdims=True) * inv_n
        yn = dlt * lax.rsqrt(var + RWKV_GN_EPS)
        y_p = (yn[0:L] + yn[L:2 * L]) * gnw_ref[:, sl] + gnb_ref[:, sl]
        bonus = headsum(r[:, sl] * kmod[:, sl] * rk_ref[:, sl]) * v[:, sl]
        o_ref[:, sl] = ((y_p + bonus) * g[:, sl]).astype(o_ref.dtype)


def _rwkv(p_rw, w_decay_up, decay_bias, w_aicl_up, aicl_bias, w_gate_up, k_k, k_a, r_k, gn_w, gn_b, *, seq):
    T, C = p_rw.shape
    W = w_decay_up.shape[1]
    B = T // seq
    L = RWKV_CHUNK
    nC = seq // L
    row = lambda t: t.reshape(1, W)
    full = lambda shape: pl.BlockSpec(shape, lambda b, c: (0,) * len(shape))
    return pl.pallas_call(
        functools.partial(_rwkv_kernel, width=W),
        grid=(B, nC),
        in_specs=[pl.BlockSpec((L, C), lambda b, c: (b * nC + c, 0)),
                  full((DECAY_RANK, W)), full((1, W)), full((AICL_RANK, W)), full((1, W)),
                  full((GATE_RANK, W)), full((1, W)), full((1, W)), full((1, W)), full((1, W)), full((1, W))],
        out_specs=pl.BlockSpec((L, W), lambda b, c: (b * nC + c, 0)),
        out_shape=jax.ShapeDtypeStruct((T, W), BF16),
        scratch_shapes=[pltpu.VMEM((W // LANES, LANES, LANES), F32)],
        compiler_params=_cparams(("arbitrary", "arbitrary")),
        name="rwkv",
    )(p_rw, w_decay_up, row(decay_bias), w_aicl_up, row(aicl_bias), w_gate_up, row(k_k), row(k_a),
      row(r_k), row(gn_w), row(gn_b))


def _t5_bucket(rel):
    n = jnp.maximum(rel, 0)
    max_exact = N_BUCKETS // 2
    nf = jnp.maximum(n, 1).astype(F32)
    large = max_exact + (jnp.log(nf / max_exact) / math.log(MAX_DISTANCE / max_exact)
                         * (N_BUCKETS - max_exact)).astype(jnp.int32)
    large = jnp.minimum(large, N_BUCKETS - 1)
    return jnp.where(n < max_exact, n, large)


def _attn_kernel(q_ref, k_ref, v_ref, bd_ref, bn_ref, lq1_ref, lk1_ref, lq2_ref, lk2_ref,
                 sw_ref, o_ref, qs_ref, m_ref, l_ref, acc_ref, *, tq, rsub, lambda_init):
    qi = pl.program_id(2)
    tk = tq

    q = q_ref[...]
    lane = lax.broadcasted_iota(jnp.int32, q.shape, 1)
    zero = jnp.zeros_like(q)
    qs_ref[0:tq, :] = jnp.where(lane < DIFF_HEAD_DIM, q, zero)
    qs_ref[tq:2 * tq, :] = jnp.where(lane < DIFF_HEAD_DIM, zero, q)
    m_ref[...] = jnp.full_like(m_ref, MASK_VALUE)
    l_ref[...] = jnp.zeros_like(l_ref)
    acc_ref[...] = jnp.zeros_like(acc_ref)

    def tile(kj, bias_ref, width=tk):
        koff = pl.multiple_of(kj * tk, tk)
        k = k_ref[pl.ds(koff, width), :]
        v = v_ref[pl.ds(koff, width), :]
        subs = [slice(sb * rsub, (sb + 1) * rsub) for sb in range(2 * tq // rsub)]
        m_prev = [m_ref[rows, :] for rows in subs]
        l_prev = [l_ref[rows, :] for rows in subs]
        acc_prev = [acc_ref[rows, :] for rows in subs]
        s_all = [_dot_nt(qs_ref[rows, :], k) for rows in subs]
        m_out, l_out, alphas, ps = [], [], [], []
        for sb, s in enumerate(s_all):
            if bias_ref is not None:
                b0 = (sb * rsub) % tq
                s = s + bias_ref[0, b0:b0 + rsub, :]
            m_new = jnp.maximum(m_prev[sb], jnp.max(s, axis=-1, keepdims=True))
            alpha = jnp.exp2(m_prev[sb] - m_new)
            p = jnp.exp2(s - jnp.concatenate([m_new] * (width // LANES), axis=1))
            m_out.append(m_new)
            l_out.append(alpha * l_prev[sb] + jnp.sum(p, axis=-1, keepdims=True))
            alphas.append(alpha)
            ps.append(p.astype(BF16))
        pv = [_dot(p, v) for p in ps]
        for sb, rows in enumerate(subs):
            m_ref[rows, :] = m_out[sb]
            l_ref[rows, :] = l_out[sb]
            acc_ref[rows, :] = alphas[sb] * acc_prev[sb] + pv[sb]

    n_far = jnp.maximum(qi - 1, 0)

    def far_pair(j, carry):
        tile(2 * j, None, width=2 * tk)
        return carry

    lax.fori_loop(0, n_far // 2, far_pair, 0)

    @pl.when(n_far % 2 == 1)
    def _():
        tile(n_far - 1, None)

    @pl.when(qi >= 1)
    def _():
        tile(qi - 1, bn_ref)

    tile(qi, bd_ref)
    l = l_ref[...]
    acc = acc_ref[...]
    o1 = acc[0:tq] / l[0:tq]
    o2 = acc[tq:2 * tq] / l[tq:2 * tq]
    lam = (jnp.exp(jnp.sum(lq1_ref[...] * lk1_ref[...], axis=-1, keepdims=True))
           - jnp.exp(jnp.sum(lq2_ref[...] * lk2_ref[...], axis=-1, keepdims=True)) + lambda_init)
    o = o1 - lam * o2
    o = o * lax.rsqrt(jnp.mean(o * o, axis=-1, keepdims=True) + SUBLN_EPS)
    o = o * sw_ref[...] * (1.0 - lambda_init)
    o_ref[...] = o.astype(o_ref.dtype)


def _diff_attention(p_da, rel_bias, lam_q1, lam_k1, lam_q2, lam_k2, subln_w, lambda_init, *, seq, tq, rsub):
    T, C = p_da.shape
    H = DIFF_HEADS
    B = T // seq
    nq = seq // tq
    hd2 = 2 * DIFF_HEAD_DIM
    far = rel_bias[N_BUCKETS - 1]
    L = 2 * tq
    f = ((rel_bias[_t5_bucket(jnp.arange(L))] - far) * LOG2E).T.astype(F32)
    fill = lambda n, val: jnp.full((H, n), val, F32)
    w_d = jnp.concatenate([f[:, 0:1], fill(tq - 1, MASK_VALUE), fill(1, 0.0), f[:, 1:tq][:, ::-1]], axis=1)
    w_n = jnp.concatenate([f[:, 1:tq + 1][:, ::-1], fill(1, 0.0), f[:, tq + 1:L][:, ::-1]], axis=1)
    toeplitz = lambda w: jnp.tile(w, (1, tq))[:, :tq * (L - 1)].reshape(H, tq, L - 1)[:, :, :tq]
    bias_d = toeplitz(w_d)
    bias_n = toeplitz(w_n)
    vec = lambda t: t.reshape(1, -1)
    small = lambda n: pl.BlockSpec((1, n), lambda b, h, i: (0, 0))
    return pl.pallas_call(
        functools.partial(_attn_kernel, tq=tq, rsub=rsub, lambda_init=lambda_init),
        grid=(B, H, nq),
        in_specs=[pl.BlockSpec((tq, hd2), lambda b, h, i: (b * nq + i, h)),
                  pl.BlockSpec((seq, hd2), lambda b, h, i: (b, H + h)),
                  pl.BlockSpec((seq, hd2), lambda b, h, i: (b, 2 * H + h)),
                  pl.BlockSpec((1, tq, tq), lambda b, h, i: (h, 0, 0)),
                  pl.BlockSpec((1, tq, tq), lambda b, h, i: (h, 0, 0)),
                  small(DIFF_HEAD_DIM), small(DIFF_HEAD_DIM), small(DIFF_HEAD_DIM), small(DIFF_HEAD_DIM),
                  small(hd2)],
        out_specs=pl.BlockSpec((tq, hd2), lambda b, h, i: (b * nq + i, h)),
        out_shape=jax.ShapeDtypeStruct((T, H * hd2), BF16),
        scratch_shapes=[pltpu.VMEM((2 * tq, hd2), BF16), pltpu.VMEM((2 * tq, LANES), F32),
                        pltpu.VMEM((2 * tq, LANES), F32), pltpu.VMEM((2 * tq, hd2), F32)],
        compiler_params=_cparams(("arbitrary", "arbitrary", "arbitrary")),
        name="diff_attn",
    )(p_da, p_da, p_da, bias_d, bias_n, vec(lam_q1), vec(lam_k1), vec(lam_q2), vec(lam_k2), vec(subln_w))


def _outproj_kernel(orw_ref, oda_ref, x_ref, w1_ref, w2_ref, pg_ref, ga_ref, fg_ref, scf_ref, shf_ref,
                    wr_ref, br_ref, x1_ref, h2_ref, route_ref):
    mix = _dot(orw_ref[...], w1_ref[...]) + _dot(oda_ref[...], w2_ref[...])
    y = mix * lax.rsqrt(jnp.mean(mix * mix, axis=-1, keepdims=True) + NORM_EPS) * pg_ref[...]
    x1 = x_ref[...] + ga_ref[0] * y
    x1_ref[...] = x1
    h2 = x1 * lax.rsqrt(jnp.mean(x1 * x1, axis=-1, keepdims=True) + NORM_EPS) * fg_ref[...]
    h2 = h2 * (1.0 + scf_ref[0]) + shf_ref[0]
    half = h2.shape[1] // 2
    hi = lax.bitcast_convert_type(h2[:, :half].astype(BF16).astype(F32), jnp.uint32)
    lo = lax.bitcast_convert_type(h2[:, half:].astype(BF16).astype(F32), jnp.uint32)
    h2_ref[...] = hi | (lo >> 16)

    logits = jnp.dot(h2, wr_ref[...], preferred_element_type=F32, precision=lax.Precision.HIGHEST) + br_ref[...]
    lane = lax.broadcasted_iota(jnp.int32, logits.shape, 1).astype(F32)
    big = float(LANES)
    neg = -jnp.inf
    cl = jnp.where(lane < N_GROUPS, logits, neg)
    cmax = jnp.max(cl, axis=-1, keepdims=True)
    grp = jnp.min(jnp.where(cl == cmax, lane, big), axis=-1, keepdims=True)
    grp_p = 1.0 / jnp.sum(jnp.exp(cl - cmax), axis=-1, keepdims=True)
    f_lo = N_GROUPS + EXPERTS_PER_GROUP * grp
    fl = jnp.where((lane >= f_lo) & (lane < f_lo + EXPERTS_PER_GROUP), logits, neg)
    v1 = jnp.max(fl, axis=-1, keepdims=True)
    i1 = jnp.min(jnp.where(fl == v1, lane, big), axis=-1, keepdims=True)
    fl2 = jnp.where(lane == i1, neg, fl)
    v2 = jnp.max(fl2, axis=-1, keepdims=True)
    i2 = jnp.min(jnp.where(fl2 == v2, lane, big), axis=-1, keepdims=True)
    e21 = jnp.exp(v2 - v1)
    w1 = grp_p / (1.0 + e21)
    w2 = w1 * e21
    route = jnp.where(lane == 0, i1 - N_GROUPS,
                      jnp.where(lane == 1, i2 - N_GROUPS,
                                jnp.where(lane == 2, w1, jnp.where(lane == 3, w2, 0.0))))
    route_ref[...] = route


def _outproj(o_rw, o_da, x2, w_out_bf16, post_gain, g_a, ffn_gain, sc_f, sh_f, w_route, b_route, *, seq, tm):
    T, D = x2.shape
    W1 = o_rw.shape[1]
    W2 = o_da.shape[1]
    B = T // seq
    tpb = seq // tm
    rowD = lambda: pl.BlockSpec((1, D), lambda i: (0, 0))
    perb = lambda: pl.BlockSpec((1, 1, D), lambda i: (i // tpb, 0, 0))
    r3 = lambda t: t.reshape(B, 1, D)
    return pl.pallas_call(
        _outproj_kernel,
        grid=(T // tm,),
        in_specs=[pl.BlockSpec((tm, W1), lambda i: (i, 0)),
                  pl.BlockSpec((tm, W2), lambda i: (i, 0)),
                  pl.BlockSpec((tm, D), lambda i: (i, 0)),
                  pl.BlockSpec((W1, D), lambda i: (0, 0)),
                  pl.BlockSpec((W2, D), lambda i: (1, 0)),
                  rowD(), perb(), rowD(), perb(), perb(),
                  pl.BlockSpec((D, LANES), lambda i: (0, 0)),
                  pl.BlockSpec((1, LANES), lambda i: (0, 0))],
        out_specs=[pl.BlockSpec((tm, D), lambda i: (i, 0)),
                   pl.BlockSpec((tm, D // 2), lambda i: (i, 0)),
                   pl.BlockSpec((tm, LANES), lambda i: (i, 0))],
        out_shape=[jax.ShapeDtypeStruct((T, D), F32),
                   jax.ShapeDtypeStruct((T, D // 2), jnp.uint32),
                   jax.ShapeDtypeStruct((T, LANES), F32)],
        compiler_params=_cparams(("arbitrary",)),
        name="outproj_router",
    )(o_rw, o_da, x2, w_out_bf16, w_out_bf16, post_gain.reshape(1, D), r3(g_a), ffn_gain.reshape(1, D),
      r3(sc_f), r3(sh_f), w_route, b_route)


def _new_expert(te_ref, i):
    return (i == 0) | (te_ref[i] != te_ref[jnp.maximum(i - 1, 0)])


def _expert_up_kernel(te_ref, nu_ref, x_ref, wg_ref, wu_ref, o_ref, wgb_ref, wub_ref):
    i = pl.program_id(1)

    @pl.when(i < nu_ref[0])
    def _():
        @pl.when(_new_expert(te_ref, i))
        def _():
            wgb_ref[...] = wg_ref[0].astype(BF16)
            wub_ref[...] = wu_ref[0].astype(BF16)

        xw = x_ref[...]
        half = xw.shape[1]
        xa = lax.bitcast_convert_type(xw & jnp.uint32(0xFFFF0000), F32).astype(BF16)
        xb = lax.bitcast_convert_type(xw << 16, F32).astype(BF16)
        gt = _dot(xa, wgb_ref[0:half, :]) + _dot(xb, wgb_ref[half:2 * half, :])
        up = _dot(xa, wub_ref[0:half, :]) + _dot(xb, wub_ref[half:2 * half, :])
        o_ref[...] = (gt * jax.nn.sigmoid(gt) * up).astype(o_ref.dtype)

    @pl.when(i >= nu_ref[0])
    def _():
        o_ref[...] = jnp.zeros_like(o_ref)


def _expert_down_kernel(te_ref, nu_ref, a_ref, wd_ref, o_ref, wdb_ref):
    i = pl.program_id(1)

    @pl.when(i < nu_ref[0])
    def _():
        @pl.when(_new_expert(te_ref, i))
        def _():
            wdb_ref[...] = wd_ref[0].astype(BF16)

        y = _dot(a_ref[...], wdb_ref[...])
        q = y.shape[1] // 2
        hi = lax.bitcast_convert_type(y[:, :q].astype(BF16).astype(F32), jnp.uint32)
        lo = lax.bitcast_convert_type(y[:, q:].astype(BF16).astype(F32), jnp.uint32)
        o_ref[...] = hi | (lo >> 16)

    @pl.when(i >= nu_ref[0])
    def _():
        o_ref[...] = jnp.zeros_like(o_ref)


def _experts(xs, tile_expert, n_used, wg, wu, wd, *, tm, n_split=EXPERT_SPLIT):
    P = xs.shape[0]
    D, F = wg.shape[1], wg.shape[2]
    n_tiles = P // tm
    fh = F // n_split
    dh = D // n_split
    row = lambda j, i, te, nu: jnp.minimum(i, nu[0] - 1)
    act = pl.pallas_call(
        _expert_up_kernel,
        grid_spec=pltpu.PrefetchScalarGridSpec(
            num_scalar_prefetch=2,
            grid=(n_split, n_tiles),
            in_specs=[pl.BlockSpec((tm, D // 2), lambda j, i, te, nu: (row(j, i, te, nu), 0)),
                      pl.BlockSpec((1, D, fh), lambda j, i, te, nu: (te[i], 0, j)),
                      pl.BlockSpec((1, D, fh), lambda j, i, te, nu: (te[i], 0, j))],
            out_specs=pl.BlockSpec((tm, fh), lambda j, i, te, nu: (i, j)),
            scratch_shapes=[pltpu.VMEM((D, fh), BF16), pltpu.VMEM((D, fh), BF16)],
        ),
        out_shape=jax.ShapeDtypeStruct((P, F), BF16),
        compiler_params=_cparams(("arbitrary", "arbitrary")),
        name="experts_up",
    )(tile_expert, n_used, xs, wg, wu)
    return pl.pallas_call(
        _expert_down_kernel,
        grid_spec=pltpu.PrefetchScalarGridSpec(
            num_scalar_prefetch=2,
            grid=(n_split, n_tiles),
            in_specs=[pl.BlockSpec((tm, F), lambda j, i, te, nu: (row(j, i, te, nu), 0)),
                      pl.BlockSpec((1, F, dh), lambda j, i, te, nu: (te[i], 0, j))],
            out_specs=pl.BlockSpec((tm, dh // 2), lambda j, i, te, nu: (i, j)),
            scratch_shapes=[pltpu.VMEM((F, dh), BF16)],
        ),
        out_shape=jax.ShapeDtypeStruct((P, D // 2), jnp.uint32),
        compiler_params=_cparams(("arbitrary", "arbitrary")),
        name="experts_down",
    )(tile_expert, n_used, act, wd)


def _row_copy(src_ref, row, dst_ref, slot, sem):
    return pltpu.make_async_copy(src_ref.at[row], dst_ref.at[slot], sem)


def _wait_rows(src_ref, dst_ref, sem):
    pltpu.make_async_copy(src_ref.at[pl.ds(0, dst_ref.shape[0])], dst_ref, sem).wait()


def _gather_kernel(ns_ref, idx_ref, src_ref, o_ref, sem):
    i = pl.program_id(0)

    @pl.when(i < ns_ref[0])
    def _():
        def issue(r, carry):
            _row_copy(src_ref, idx_ref[r], o_ref, r, sem).start()
            return carry

        lax.fori_loop(0, o_ref.shape[0], issue, 0, unroll=8)
        _wait_rows(src_ref, o_ref, sem)

    @pl.when(i >= ns_ref[0])
    def _():
        o_ref[...] = jnp.zeros_like(o_ref)


def _gather_rows(src, idx, n_steps_used, *, tg):
    P = idx.shape[0]
    D = src.shape[1]
    out_dtype = src.dtype
    return pl.pallas_call(
        _gather_kernel,
        grid_spec=pltpu.PrefetchScalarGridSpec(
            num_scalar_prefetch=1,
            grid=(P // tg,),
            in_specs=[pl.BlockSpec((tg,), lambda i, ns: (i,), memory_space=pltpu.SMEM),
                      pl.BlockSpec(memory_space=pl.ANY)],
            out_specs=pl.BlockSpec((tg, D), lambda i, ns: (i, 0)),
            scratch_shapes=[pltpu.SemaphoreType.DMA(())],
        ),
        out_shape=jax.ShapeDtypeStruct((P, D), out_dtype),
        compiler_params=pltpu.CompilerParams(dimension_semantics=("arbitrary",), vmem_limit_bytes=VMEM_LIMIT,
                                             disable_bounds_checks=True),
        name="dispatch_gather",
    )(n_steps_used, idx, src)


def _combine_kernel(pos_ref, ys_ref, route_ref, x1_ref, pg_ref, gf_ref, o_ref, y0_ref, y1_ref, sem0, sem1, *,
                    n_split):
    tm = x1_ref.shape[0]

    def issue(t, carry):
        _row_copy(ys_ref, pos_ref[2 * t], y0_ref, t, sem0).start()
        _row_copy(ys_ref, pos_ref[2 * t + 1], y1_ref, t, sem1).start()
        return carry

    lax.fori_loop(0, tm, issue, 0, unroll=8)
    route = route_ref[...]
    w0 = route[:, 2:3]
    w1 = route[:, 3:4]
    _wait_rows(ys_ref, y0_ref, sem0)
    _wait_rows(ys_ref, y1_ref, sem1)
    hi_of = lambda w: lax.bitcast_convert_type(w & jnp.uint32(0xFFFF0000), F32)
    lo_of = lambda w: lax.bitcast_convert_type(w << 16, F32)
    p0 = y0_ref[...]
    p1 = y1_ref[...]
    y_hi = hi_of(p0) * w0 + hi_of(p1) * w1
    y_lo = lo_of(p0) * w0 + lo_of(p1) * w1
    d_model = x1_ref.shape[1]
    ms = (jnp.sum(y_hi * y_hi, axis=-1, keepdims=True) + jnp.sum(y_lo * y_lo, axis=-1, keepdims=True)) / d_model
    scale = lax.rsqrt(ms + NORM_EPS)
    q = d_model // (2 * n_split)
    for j in range(n_split):
        for part, y_part in enumerate((y_hi, y_lo)):
            cols = slice((2 * j + part) * q, (2 * j + part + 1) * q)
            yn = y_part[:, j * q:(j + 1) * q] * scale * pg_ref[:, cols]
            o_ref[:, cols] = x1_ref[:, cols] + gf_ref[0, :, cols] * yn


def _combine(ys, pos, route, x1, post_gain, g_f, *, seq, tm):
    T, D = x1.shape
    B = T // seq
    tpb = seq // tm
    return pl.pallas_call(
        functools.partial(_combine_kernel, n_split=EXPERT_SPLIT),
        grid=(T // tm,),
        in_specs=[pl.BlockSpec((2 * tm,), lambda i: (i,), memory_space=pltpu.SMEM),
                  pl.BlockSpec(memory_space=pl.ANY),
                  pl.BlockSpec((tm, LANES), lambda i: (i, 0)),
                  pl.BlockSpec((tm, D), lambda i: (i, 0)),
                  pl.BlockSpec((1, D), lambda i: (0, 0)),
                  pl.BlockSpec((1, 1, D), lambda i: (i // tpb, 0, 0))],
        out_specs=pl.BlockSpec((tm, D), lambda i: (i, 0)),
        out_shape=jax.ShapeDtypeStruct((T, D), F32),
        scratch_shapes=[pltpu.VMEM((tm, D // 2), ys.dtype), pltpu.VMEM((tm, D // 2), ys.dtype),
                        pltpu.SemaphoreType.DMA(()), pltpu.SemaphoreType.DMA(())],
        compiler_params=pltpu.CompilerParams(dimension_semantics=("arbitrary",), vmem_limit_bytes=VMEM_LIMIT,
                                             disable_bounds_checks=True),
        name="combine",
    )(pos, ys, route, x1, post_gain.reshape(1, D), g_f.reshape(B, 1, D))


def _dispatch_plan(expert_flat, tm, n_tiles):
    n = expert_flat.shape[0]
    onehot = (expert_flat[:, None] == jnp.arange(N_EXPERTS, dtype=jnp.int32)[None, :]).astype(jnp.int32)
    csum = jnp.cumsum(onehot, axis=0)
    counts = csum[-1]
    rank = jnp.sum((csum - onehot) * onehot, axis=1)
    tiles_e = (counts + tm - 1) // tm
    tile_end = jnp.cumsum(tiles_e)
    tile_start = tile_end - tiles_e
    pos = tile_start[expert_flat] * tm + rank
    n_used = tile_end[-1]
    tile_ids = jnp.minimum(jnp.arange(n_tiles, dtype=jnp.int32), n_used - 1)
    tile_expert = jnp.minimum(jnp.searchsorted(tile_end, tile_ids, side="right"), N_EXPERTS - 1).astype(jnp.int32)
    row_token = jnp.zeros((n_tiles * tm,), jnp.int32).at[pos].set(jnp.arange(n, dtype=jnp.int32) // 2)
    return pos, row_token, tile_expert, n_used.astype(jnp.int32).reshape(1)


def _layer(x, c, rel_bias, w_mod, b_mod, attn_pre_gain, attn_post_gain, w_in, shift_mu, w_decay_up, decay_bias,
           w_aicl_up, aicl_bias, w_gate_up, k_k, k_a, r_k, gn_w, gn_b, lam_q1, lam_k1, lam_q2, lam_k2, subln_w,
           w_out, ffn_pre_gain, ffn_post_gain, w_coarse, b_coarse, w_fine, b_fine, w_exp_gate, w_exp_up,
           w_exp_down, *, layer_index, tm_proj, tq, rsub_attn, tm_moe):
    B, S, D = x.shape
    T = B * S
    W = w_decay_up.shape[1]
    rw_cols = shift_mu.shape[0]
    lambda_init = 0.8 - 0.6 * math.exp(-0.3 * layer_index)
    x2 = x.reshape(T, D)

    mod = _mod(c, w_mod, b_mod)
    sh_a, sc_a, g_a, sh_f, sc_f, g_f = jnp.split(mod, 6, axis=-1)

    q_cols = DIFF_HEADS * 2 * DIFF_HEAD_DIM
    q_scale = DIFF_HEAD_DIM ** -0.5 * LOG2E
    w_rw_b = w_in[:, :rw_cols].astype(BF16)
    w_da_b = jnp.concatenate([w_in[:, rw_cols:rw_cols + q_cols] * q_scale, w_in[:, rw_cols + q_cols:]],
                             axis=1).astype(BF16)
    p_rw = _inproj(x2, attn_pre_gain, sc_a, sh_a, w_rw_b, shift_mu,
                   shift=True, out_dtype=F32, seq=S, tm=tm_proj)
    p_da = _inproj(x2, attn_pre_gain, sc_a, sh_a, w_da_b, jnp.zeros((w_in.shape[1] - rw_cols,), F32),
                   shift=False, out_dtype=BF16, seq=S, tm=tm_proj)

    o_rw = _rwkv(p_rw, w_decay_up, decay_bias, w_aicl_up, aicl_bias, w_gate_up, k_k, k_a, r_k.reshape(-1),
                 gn_w, gn_b, seq=S)
    o_da = _diff_attention(p_da, rel_bias, lam_q1, lam_k1, lam_q2, lam_k2, subln_w, lambda_init, seq=S, tq=tq,
                           rsub=rsub_attn)

    w_route = jnp.zeros((D, LANES), F32).at[:, :N_GROUPS].set(w_coarse).at[:, N_GROUPS:N_GROUPS + N_EXPERTS].set(w_fine)
    b_route = jnp.zeros((1, LANES), F32).at[0, :N_GROUPS].set(b_coarse).at[0, N_GROUPS:N_GROUPS + N_EXPERTS].set(b_fine)
    x1, h2, route = _outproj(o_rw, o_da, x2, w_out.astype(BF16), attn_post_gain, g_a, ffn_pre_gain, sc_f, sh_f,
                             w_route, b_route, seq=S, tm=tm_proj)

    expert_flat = route[:, 0:2].astype(jnp.int32).reshape(-1)
    n_tiles = (2 * T) // tm_moe + N_EXPERTS
    pos, row_token, tile_expert, n_used = _dispatch_plan(expert_flat, tm_moe, n_tiles)
    gather_steps_used = (n_used * tm_moe + GATHER_ROWS - 1) // GATHER_ROWS
    xs = _gather_rows(h2, row_token, gather_steps_used, tg=GATHER_ROWS)
    ys = _experts(xs, tile_expert, n_used, w_exp_gate, w_exp_up, w_exp_down, tm=tm_moe)
    out = _combine(ys, pos, route, x1, ffn_post_gain, g_f, seq=S, tm=COMBINE_ROWS)
    return out.reshape(B, S, D)


def kernel(x, c, rel_bias, w_mod, b_mod, attn_pre_gain, attn_post_gain, w_in, shift_mu, w_decay_up, decay_bias,
           w_aicl_up, aicl_bias, w_gate_up, k_k, k_a, r_k, gn_w, gn_b, lam_q1, lam_k1, lam_q2, lam_k2, subln_w,
           w_out, ffn_pre_gain, ffn_post_gain, w_coarse, b_coarse, w_fine, b_fine, w_exp_gate, w_exp_up,
           w_exp_down):
    depth = w_mod.shape[0]
    S = x.shape[1]
    tm_proj = min(256, S)
    tq = min(512, S)
    tm_moe = 512 if S >= 4096 else 128
    for l in range(depth):
        x = _layer(x, c, rel_bias, w_mod[l], b_mod[l], attn_pre_gain[l], attn_post_gain[l], w_in[l], shift_mu[l],
                   w_decay_up[l], decay_bias[l], w_aicl_up[l], aicl_bias[l], w_gate_up[l], k_k[l], k_a[l], r_k[l],
                   gn_w[l], gn_b[l], lam_q1[l], lam_k1[l], lam_q2[l], lam_k2[l], subln_w[l], w_out[l],
                   ffn_pre_gain[l], ffn_post_gain[l], w_coarse[l], b_coarse[l], w_fine[l], b_fine[l],
                   w_exp_gate[l], w_exp_up[l], w_exp_down[l], layer_index=l, tm_proj=tm_proj, tq=tq,
                   rsub_attn=min(256, tq), tm_moe=tm_moe)
    return x
```

```python
import functools
import math

import jax
import jax.numpy as jnp
from jax import lax
from jax.experimental import pallas as pl
from jax.experimental.pallas import tpu as pltpu

F32 = jnp.float32
BF16 = jnp.bfloat16

RWKV_HEAD_DIM = 64
DECAY_RANK = 64
AICL_RANK = 64
GATE_RANK = 128
RWKV_GN_EPS = 64e-5
DIFF_HEADS = 8
DIFF_HEAD_DIM = 64
N_BUCKETS = 32
MAX_DISTANCE = 128
SUBLN_EPS = 1e-5
N_GROUPS = 8
EXPERTS_PER_GROUP = 8
N_EXPERTS = N_GROUPS * EXPERTS_PER_GROUP
NORM_EPS = 1e-6
MASK_VALUE = -1e30
LOG2E = math.log2(math.e)

LANES = 128
RWKV_CHUNK = 64
VMEM_LIMIT = 56 * 1024 * 1024
GATHER_ROWS = 1024
COMBINE_ROWS = 512
ISSUE_UNROLL = 8
EXPERT_SPLIT = 2


def _cparams(sem):
    return pltpu.CompilerParams(dimension_semantics=sem, vmem_limit_bytes=VMEM_LIMIT)


def _dot(a, b):
    return jnp.dot(a, b, preferred_element_type=F32)


def _dot_nt(a, b):
    return lax.dot_general(a, b, (((1,), (1,)), ((), ())), preferred_element_type=F32)


def _dot_tn(a, b):
    return lax.dot_general(a, b, (((0,), (0,)), ((), ())), preferred_element_type=F32)


def _mod_kernel(c_ref, w_ref, b_ref, o_ref):
    c = c_ref[...]
    s = c * jax.nn.sigmoid(c)
    o_ref[...] = _dot(s, w_ref[...]) + b_ref[...]


def _mod(c, w_mod, b_mod, tn=1024):
    B, D = c.shape
    N = w_mod.shape[1]
    rows = 8
    c8 = jnp.zeros((rows, D), F32).at[:B].set(c)
    out = pl.pallas_call(
        _mod_kernel,
        grid=(N // tn,),
        in_specs=[pl.BlockSpec((rows, D), lambda j: (0, 0)),
                  pl.BlockSpec((D, tn), lambda j: (0, j)),
                  pl.BlockSpec((1, tn), lambda j: (0, j))],
        out_specs=pl.BlockSpec((rows, tn), lambda j: (0, j)),
        out_shape=jax.ShapeDtypeStruct((rows, N), F32),
        compiler_params=_cparams(("arbitrary",)),
        name="mod",
    )(c8, w_mod, b_mod.reshape(1, N))
    return out[:B]


def _inproj_kernel(x_ref, gain_ref, sc_ref, sh_ref, w_ref, mu_ref, o_ref, carry_ref, *, shift, tiles_per_batch):
    i = pl.program_id(0)
    x = x_ref[...]
    ms = jnp.mean(x * x, axis=-1, keepdims=True)
    h = x * lax.rsqrt(ms + NORM_EPS) * gain_ref[...]
    h = h * (1.0 + sc_ref[0]) + sh_ref[0]
    p = _dot(h.astype(BF16), w_ref[...])
    if shift:
        tm = p.shape[0]

        @pl.when(i % tiles_per_batch == 0)
        def _():
            carry_ref[...] = jnp.zeros_like(carry_ref)

        prev = pltpu.roll(p, 1, axis=0)
        row = lax.broadcasted_iota(jnp.int32, p.shape, 0)
        prev = jnp.where(row == 0, carry_ref[0:1, :], prev)
        carry_ref[0:1, :] = p[tm - 1:tm, :]
        p = p + (prev - p) * mu_ref[...]
    o_ref[...] = p.astype(o_ref.dtype)


def _inproj(x2, gain, sc, sh, w_bf16, mu, *, shift, out_dtype, seq, tm):
    T, D = x2.shape
    N = w_bf16.shape[1]
    B = T // seq
    tiles_per_batch = seq // tm
    kern = functools.partial(_inproj_kernel, shift=shift, tiles_per_batch=tiles_per_batch)
    return pl.pallas_call(
        kern,
        grid=(T // tm,),
        in_specs=[pl.BlockSpec((tm, D), lambda i: (i, 0)),
                  pl.BlockSpec((1, D), lambda i: (0, 0)),
                  pl.BlockSpec((1, 1, D), lambda i: (i // tiles_per_batch, 0, 0)),
                  pl.BlockSpec((1, 1, D), lambda i: (i // tiles_per_batch, 0, 0)),
                  pl.BlockSpec((D, N), lambda i: (0, 0)),
                  pl.BlockSpec((1, N), lambda i: (0, 0))],
        out_specs=pl.BlockSpec((tm, N), lambda i: (i, 0)),
        out_shape=jax.ShapeDtypeStruct((T, N), out_dtype),
        scratch_shapes=[pltpu.VMEM((8, N), F32)],
        compiler_params=_cparams(("arbitrary",)),
        name="inproj_shift" if shift else "inproj",
    )(x2, gain.reshape(1, D), sc.reshape(B, 1, D), sh.reshape(B, 1, D), w_bf16, mu.reshape(1, N))


def _rwkv_kernel(p_ref, wdu_ref, db_ref, wau_ref, ab_ref, wgu_ref, kk_ref, ka_ref, rk_ref,
                 gnw_ref, gnb_ref, o_ref, s_ref, *, width):
    L = RWKV_CHUNK
    W = width
    n_pairs = W // LANES
    c = pl.program_id(1)

    @pl.when(c == 0)
    def _():
        s_ref[...] = jnp.zeros_like(s_ref)

    r = p_ref[:, 0:W]
    k = p_ref[:, W:2 * W]
    v = p_ref[:, 2 * W:3 * W]
    xw = p_ref[:, 3 * W:3 * W + DECAY_RANK]
    xa = p_ref[:, 3 * W + DECAY_RANK:3 * W + DECAY_RANK + AICL_RANK]
    xg = p_ref[:, 3 * W + DECAY_RANK + AICL_RANK:3 * W + DECAY_RANK + AICL_RANK + GATE_RANK]

    d = db_ref[...] + _dot(jnp.tanh(xw).astype(BF16), wdu_ref[...].astype(BF16))
    lw = -math.exp(-0.5) * jax.nn.sigmoid(d)
    a = jax.nn.sigmoid(ab_ref[...] + _dot(xa.astype(BF16), wau_ref[...].astype(BF16)))
    g = _dot(jax.nn.sigmoid(xg).astype(BF16), wgu_ref[...].astype(BF16))
    kkf = k * kk_ref[...]
    kmod = k * (1.0 + (a - 1.0) * ka_ref[...])

    ri = lax.broadcasted_iota(jnp.int32, (L, L), 0)
    ci = lax.broadcasted_iota(jnp.int32, (L, L), 1)
    tri = (ri >= ci).astype(BF16)
    lw_hi = lw.astype(BF16)
    rem = lw - lw_hi.astype(F32)
    lw_mid = rem.astype(BF16)
    lw_lo = (rem - lw_mid.astype(F32)).astype(BF16)
    cin = _dot(tri, lw_hi) + _dot(tri, lw_mid) + _dot(tri, lw_lo)
    cex = cin - lw
    c_last = cin[L - 1:L, :]
    e_in = jnp.exp(cin)
    e_ni = jnp.exp(-cin)
    e_ex = jnp.exp(cex)
    e_l = jnp.exp(c_last - cin)
    g_last = jnp.exp(c_last)

    lane = lax.broadcasted_iota(jnp.int32, (L, LANES), 1)
    lo = lane < RWKV_HEAD_DIM
    i2 = lax.broadcasted_iota(jnp.int32, (2 * L, LANES), 0)
    j2 = lax.broadcasted_iota(jnp.int32, (2 * L, LANES), 1)
    same_blk = (i2 // L) == (j2 // RWKV_HEAD_DIM)
    mask_strict = same_blk & (i2 > j2)
    mask_incl = same_blk & (i2 >= j2)
    eye = (i2 == j2).astype(F32)

    def stack(t):
        return jnp.concatenate([jnp.where(lo, t, 0.0), jnp.where(lo, 0.0, t)], axis=0)

    def dup(t):
        return jnp.concatenate([t, t], axis=0)

    def headsum(t):
        s_lo = jnp.sum(jnp.where(lo, t, 0.0), axis=-1, keepdims=True)
        s_hi = jnp.sum(jnp.where(lo, 0.0, t), axis=-1, keepdims=True)
        return jnp.where(lo, s_lo, s_hi)

    inv_n = 1.0 / RWKV_HEAD_DIM
    pairs = range(n_pairs)
    sls = [slice(hp * LANES, (hp + 1) * LANES) for hp in pairs]
    lh, rh, v_st, bk = [], [], [], []
    for sl in sls:
        kk_p = kkf[:, sl]
        nrm = jnp.maximum(jnp.sqrt(headsum(kk_p * kk_p)), 1e-12)
        kn = kk_p / nrm
        bp = kn * a[:, sl]
        ag = stack(-kn * e_ex[:, sl])
        rg = stack(r[:, sl] * e_in[:, sl])
        bd = dup(bp * e_ni[:, sl])
        kd = dup(kmod[:, sl] * e_ni[:, sl])
        bl = stack(bp * e_l[:, sl])
        kl = stack(kmod[:, sl] * e_l[:, sl])
        v_st.append(stack(v[:, sl]).astype(BF16))
        lh.append(jnp.concatenate([ag, rg], axis=0).astype(BF16))
        rh.append(jnp.concatenate([bd, kd], axis=0).astype(BF16))
        bk.append(jnp.concatenate([bl, kl], axis=0).astype(BF16))
    a4 = [_dot_nt(lh[i], rh[i]) for i in pairs]
    a_ab = [jnp.where(mask_strict, a4[i][0:2 * L, 0:2 * L], 0.0) for i in pairs]
    a_ak = [jnp.where(mask_strict, a4[i][0:2 * L, 2 * L:4 * L], 0.0).astype(BF16) for i in pairs]
    a_rb = [jnp.where(mask_incl, a4[i][2 * L:4 * L, 0:2 * L], 0.0).astype(BF16) for i in pairs]
    a_rk = [jnp.where(mask_incl, a4[i][2 * L:4 * L, 2 * L:4 * L], 0.0).astype(BF16) for i in pairs]

    pw = a_ab
    tinv = [eye + a_ab[i] for i in pairs]
    for _ in range(int(math.log2(L)) - 1):
        pb = [pw[i].astype(BF16) for i in pairs]
        pw = [_dot(pb[i], pb[i]) for i in pairs]
        tinv = [tinv[i] + _dot(tinv[i].astype(BF16), pw[i].astype(BF16)) for i in pairs]

    s0 = [s_ref[i] for i in pairs]
    xs = [_dot_nt(lh[i], s0[i].astype(BF16)) for i in pairs]
    akv = [_dot(a_ak[i], v_st[i]) for i in pairs]
    u_b = [_dot(tinv[i].astype(BF16), (xs[i][0:2 * L] + akv[i]).astype(BF16)).astype(BF16) for i in pairs]
    y = [xs[i][2 * L:4 * L] + _dot(a_rb[i], u_b[i]) + _dot(a_rk[i], v_st[i]) for i in pairs]
    for i in pairs:
        uv = jnp.concatenate([u_b[i], v_st[i]], axis=0)
        s_ref[i] = s0[i] * g_last[:, sls[i]] + _dot_tn(uv, bk[i])

    for i, sl in enumerate(sls):
        mu = jnp.sum(y[i], axis=-1, keepdims=True) * inv_n
        dlt = jnp.where(same_blk, y[i] - mu, 0.0)
        var = jnp.sum(dlt * dlt, axis=-1, keepdims=True) * inv_n
        yn = dlt * lax.rsqrt(var + RWKV_GN_EPS)
        y_p = (yn[0:L] + yn[L:2 * L]) * gnw_ref[:, sl] + gnb_ref[:, sl]
        bonus = headsum(r[:, sl] * kmod[:, sl] * rk_ref[:, sl]) * v[:, sl]
        o_ref[:, sl] = ((y_p + bonus) * g[:, sl]).astype(o_ref.dtype)


def _rwkv(p_rw, w_decay_up, decay_bias, w_aicl_up, aicl_bias, w_gate_up, k_k, k_a, r_k, gn_w, gn_b, *, seq):
    T, C = p_rw.shape
    W = w_decay_up.shape[1]
    B = T // seq
    L = RWKV_CHUNK
    nC = seq // L
    row = lambda t: t.reshape(1, W)
    full = lambda shape: pl.BlockSpec(shape, lambda b, c: (0,) * len(shape))
    return pl.pallas_call(
        functools.partial(_rwkv_kernel, width=W),
        grid=(B, nC),
        in_specs=[pl.BlockSpec((L, C), lambda b, c: (b * nC + c, 0)),
                  full((DECAY_RANK, W)), full((1, W)), full((AICL_RANK, W)), full((1, W)),
                  full((GATE_RANK, W)), full((1, W)), full((1, W)), full((1, W)), full((1, W)), full((1, W))],
        out_specs=pl.BlockSpec((L, W), lambda b, c: (b * nC + c, 0)),
        out_shape=jax.ShapeDtypeStruct((T, W), BF16),
        scratch_shapes=[pltpu.VMEM((W // LANES, LANES, LANES), F32)],
        compiler_params=_cparams(("arbitrary", "arbitrary")),
        name="rwkv",
    )(p_rw, w_decay_up, row(decay_bias), w_aicl_up, row(aicl_bias), w_gate_up, row(k_k), row(k_a),
      row(r_k), row(gn_w), row(gn_b))


def _t5_bucket(rel):
    n = jnp.maximum(rel, 0)
    max_exact = N_BUCKETS // 2
    nf = jnp.maximum(n, 1).astype(F32)
    large = max_exact + (jnp.log(nf / max_exact) / math.log(MAX_DISTANCE / max_exact)
                         * (N_BUCKETS - max_exact)).astype(jnp.int32)
    large = jnp.minimum(large, N_BUCKETS - 1)
    return jnp.where(n < max_exact, n, large)


def _attn_kernel(q_ref, k_ref, v_ref, tb_ref, lq1_ref, lk1_ref, lq2_ref, lk2_ref,
                 sw_ref, o_ref, qs_ref, m_ref, l_ref, acc_ref, *, tq, rsub, lambda_init):
    qi = pl.program_id(2)
    tk = tq

    q = q_ref[...]
    lane = lax.broadcasted_iota(jnp.int32, q.shape, 1)
    zero = jnp.zeros_like(q)
    qs_ref[0:tq, :] = jnp.where(lane < DIFF_HEAD_DIM, q, zero)
    qs_ref[tq:2 * tq, :] = jnp.where(lane < DIFF_HEAD_DIM, zero, q)
    m_ref[...] = jnp.full_like(m_ref, MASK_VALUE)
    l_ref[...] = jnp.zeros_like(l_ref)
    acc_ref[...] = jnp.zeros_like(acc_ref)

    band = tb_ref.shape[2]

    def add_bias(s, bias):
        nb = bias.shape[1]
        if nb == s.shape[1]:
            return s + bias
        return jnp.concatenate([s[:, :s.shape[1] - nb], s[:, s.shape[1] - nb:] + bias], axis=1)

    def tile(kj, mode):
        koff = pl.multiple_of(kj * tk, tk)
        subs = [slice(sb * rsub, (sb + 1) * rsub) for sb in range(2 * tq // rsub)]
        r0s = [(sb * rsub) % tq for sb in range(len(subs))]
        widths = [r0 + rsub if mode == "diag" else tk for r0 in r0s]
        m_prev = [m_ref[rows, :] for rows in subs]
        l_prev = [l_ref[rows, :] for rows in subs]
        acc_prev = [acc_ref[rows, :] for rows in subs]
        s_all = [_dot_nt(qs_ref[rows, :], k_ref[pl.ds(koff, w), :]) for rows, w in zip(subs, widths)]
        m_out, l_out, alphas, ps = [], [], [], []
        for sb, s in enumerate(s_all):
            if mode == "diag":
                s = add_bias(s, tb_ref[0] if r0s[sb] > 0 else tb_ref[0, :, band - rsub:])
            elif mode == "near" and r0s[sb] == 0:
                s = add_bias(s, tb_ref[0, :, 0:LANES])
            m_new = jnp.maximum(m_prev[sb], jnp.max(s, axis=-1, keepdims=True))
            alpha = jnp.exp2(m_prev[sb] - m_new)
            p = jnp.exp2(s - jnp.concatenate([m_new] * (widths[sb] // LANES), axis=1))
            m_out.append(m_new)
            l_out.append(alpha * l_prev[sb] + jnp.sum(p, axis=-1, keepdims=True))
            alphas.append(alpha)
            ps.append(p.astype(BF16))
        pv = [_dot(p, v_ref[pl.ds(koff, w), :]) for p, w in zip(ps, widths)]
        for sb, rows in enumerate(subs):
            m_ref[rows, :] = m_out[sb]
            l_ref[rows, :] = l_out[sb]
            acc_ref[rows, :] = alphas[sb] * acc_prev[sb] + pv[sb]

    def far_tile(kj, carry):
        tile(kj, "far")
        return carry

    lax.fori_loop(0, jnp.maximum(qi - 1, 0), far_tile, 0)

    @pl.when(qi >= 1)
    def _():
        tile(qi - 1, "near")

    tile(qi, "diag")
    l = l_ref[...]
    acc = acc_ref[...]
    o1 = acc[0:tq] / l[0:tq]
    o2 = acc[tq:2 * tq] / l[tq:2 * tq]
    lam = (jnp.exp(jnp.sum(lq1_ref[...] * lk1_ref[...], axis=-1, keepdims=True))
           - jnp.exp(jnp.sum(lq2_ref[...] * lk2_ref[...], axis=-1, keepdims=True)) + lambda_init)
    o = o1 - lam * o2
    o = o * lax.rsqrt(jnp.mean(o * o, axis=-1, keepdims=True) + SUBLN_EPS)
    o = o * sw_ref[...] * (1.0 - lambda_init)
    o_ref[...] = o.astype(o_ref.dtype)


def _diff_attention(p_da, rel_bias, lam_q1, lam_k1, lam_q2, lam_k2, subln_w, lambda_init, *, seq, tq, rsub):
    T, C = p_da.shape
    H = DIFF_HEADS
    B = T // seq
    nq = seq // tq
    hd2 = 2 * DIFF_HEAD_DIM
    assert rsub % LANES == 0 and tq % rsub == 0 and MAX_DISTANCE <= LANES
    far = rel_bias[N_BUCKETS - 1]
    band = rsub + LANES
    L = rsub + band
    f = ((rel_bias[_t5_bucket(jnp.arange(L))] - far) * LOG2E).T.astype(F32)
    w = jnp.concatenate([f[:, 0:LANES + 1][:, ::-1], jnp.full((H, band - LANES - 1), MASK_VALUE, F32),
                         f[:, LANES + 1:LANES + rsub + 1][:, ::-1]], axis=1)
    bias_tb = jnp.tile(w, (1, rsub))[:, :rsub * (L - 1)].reshape(H, rsub, L - 1)[:, :, :band]
    vec = lambda t: t.reshape(1, -1)
    small = lambda n: pl.BlockSpec((1, n), lambda b, h, i: (0, 0))
    return pl.pallas_call(
        functools.partial(_attn_kernel, tq=tq, rsub=rsub, lambda_init=lambda_init),
        grid=(B, H, nq),
        in_specs=[pl.BlockSpec((tq, hd2), lambda b, h, i: (b * nq + i, h)),
                  pl.BlockSpec((seq, hd2), lambda b, h, i: (b, H + h)),
                  pl.BlockSpec((seq, hd2), lambda b, h, i: (b, 2 * H + h)),
                  pl.BlockSpec((1, rsub, band), lambda b, h, i: (h, 0, 0)),
                  small(DIFF_HEAD_DIM), small(DIFF_HEAD_DIM), small(DIFF_HEAD_DIM), small(DIFF_HEAD_DIM),
                  small(hd2)],
        out_specs=pl.BlockSpec((tq, hd2), lambda b, h, i: (b * nq + i, h)),
        out_shape=jax.ShapeDtypeStruct((T, H * hd2), BF16),
        scratch_shapes=[pltpu.VMEM((2 * tq, hd2), BF16), pltpu.VMEM((2 * tq, LANES), F32),
                        pltpu.VMEM((2 * tq, LANES), F32), pltpu.VMEM((2 * tq, hd2), F32)],
        compiler_params=_cparams(("arbitrary", "arbitrary", "arbitrary")),
        name="diff_attn",
    )(p_da, p_da, p_da, bias_tb, vec(lam_q1), vec(lam_k1), vec(lam_q2), vec(lam_k2), vec(subln_w))


def _outproj_kernel(orw_ref, oda_ref, x_ref, w1_ref, w2_ref, pg_ref, ga_ref, fg_ref, scf_ref, shf_ref,
                    wr_ref, br_ref, x1_ref, h2_ref, route_ref):
    mix = _dot(orw_ref[...], w1_ref[...]) + _dot(oda_ref[...], w2_ref[...])
    y = mix * lax.rsqrt(jnp.mean(mix * mix, axis=-1, keepdims=True) + NORM_EPS) * pg_ref[...]
    x1 = x_ref[...] + ga_ref[0] * y
    x1_ref[...] = x1
    h2 = x1 * lax.rsqrt(jnp.mean(x1 * x1, axis=-1, keepdims=True) + NORM_EPS) * fg_ref[...]
    h2 = h2 * (1.0 + scf_ref[0]) + shf_ref[0]
    half = h2.shape[1] // 2
    hi = lax.bitcast_convert_type(h2[:, :half].astype(BF16).astype(F32), jnp.uint32)
    lo = lax.bitcast_convert_type(h2[:, half:].astype(BF16).astype(F32), jnp.uint32)
    h2_ref[...] = hi | (lo >> 16)

    logits = jnp.dot(h2, wr_ref[...], preferred_element_type=F32, precision=lax.Precision.HIGHEST) + br_ref[...]
    lane = lax.broadcasted_iota(jnp.int32, logits.shape, 1).astype(F32)
    big = float(LANES)
    neg = -jnp.inf
    cl = jnp.where(lane < N_GROUPS, logits, neg)
    cmax = jnp.max(cl, axis=-1, keepdims=True)
    grp = jnp.min(jnp.where(cl == cmax, lane, big), axis=-1, keepdims=True)
    grp_p = 1.0 / jnp.sum(jnp.exp(cl - cmax), axis=-1, keepdims=True)
    f_lo = N_GROUPS + EXPERTS_PER_GROUP * grp
    fl = jnp.where((lane >= f_lo) & (lane < f_lo + EXPERTS_PER_GROUP), logits, neg)
    v1 = jnp.max(fl, axis=-1, keepdims=True)
    i1 = jnp.min(jnp.where(fl == v1, lane, big), axis=-1, keepdims=True)
    fl2 = jnp.where(lane == i1, neg, fl)
    v2 = jnp.max(fl2, axis=-1, keepdims=True)
    i2 = jnp.min(jnp.where(fl2 == v2, lane, big), axis=-1, keepdims=True)
    e21 = jnp.exp(v2 - v1)
    w1 = grp_p / (1.0 + e21)
    w2 = w1 * e21
    route = jnp.where(lane == 0, i1 - N_GROUPS,
                      jnp.where(lane == 1, i2 - N_GROUPS,
                                jnp.where(lane == 2, w1, jnp.where(lane == 3, w2, 0.0))))
    route_ref[...] = route


def _outproj(o_rw, o_da, x2, w_out_bf16, post_gain, g_a, ffn_gain, sc_f, sh_f, w_route, b_route, *, seq, tm):
    T, D = x2.shape
    W1 = o_rw.shape[1]
    W2 = o_da.shape[1]
    B = T // seq
    tpb = seq // tm
    rowD = lambda: pl.BlockSpec((1, D), lambda i: (0, 0))
    perb = lambda: pl.BlockSpec((1, 1, D), lambda i: (i // tpb, 0, 0))
    r3 = lambda t: t.reshape(B, 1, D)
    return pl.pallas_call(
        _outproj_kernel,
        grid=(T // tm,),
        in_specs=[pl.BlockSpec((tm, W1), lambda i: (i, 0)),
                  pl.BlockSpec((tm, W2), lambda i: (i, 0)),
                  pl.BlockSpec((tm, D), lambda i: (i, 0)),
                  pl.BlockSpec((W1, D), lambda i: (0, 0)),
                  pl.BlockSpec((W2, D), lambda i: (1, 0)),
                  rowD(), perb(), rowD(), perb(), perb(),
                  pl.BlockSpec((D, LANES), lambda i: (0, 0)),
                  pl.BlockSpec((1, LANES), lambda i: (0, 0))],
        out_specs=[pl.BlockSpec((tm, D), lambda i: (i, 0)),
                   pl.BlockSpec((tm, D // 2), lambda i: (i, 0)),
                   pl.BlockSpec((tm, LANES), lambda i: (i, 0))],
        out_shape=[jax.ShapeDtypeStruct((T, D), F32),
                   jax.ShapeDtypeStruct((T, D // 2), jnp.uint32),
                   jax.ShapeDtypeStruct((T, LANES), F32)],
        compiler_params=_cparams(("arbitrary",)),
        name="outproj_router",
    )(o_rw, o_da, x2, w_out_bf16, w_out_bf16, post_gain.reshape(1, D), r3(g_a), ffn_gain.reshape(1, D),
      r3(sc_f), r3(sh_f), w_route, b_route)


def _new_expert(te_ref, i):
    return (i == 0) | (te_ref[i] != te_ref[jnp.maximum(i - 1, 0)])


def _expert_up_kernel(te_ref, nu_ref, x_ref, wg_ref, wu_ref, o_ref, wgb_ref, wub_ref):
    i = pl.program_id(1)

    @pl.when(i < nu_ref[0])
    def _():
        @pl.when(_new_expert(te_ref, i))
        def _():
            wgb_ref[...] = wg_ref[0].astype(BF16)
            wub_ref[...] = wu_ref[0].astype(BF16)

        xw = x_ref[...]
        half = xw.shape[1]
        xa = lax.bitcast_convert_type(xw & jnp.uint32(0xFFFF0000), F32).astype(BF16)
        xb = lax.bitcast_convert_type(xw << 16, F32).astype(BF16)
        gt = _dot(xa, wgb_ref[0:half, :]) + _dot(xb, wgb_ref[half:2 * half, :])
        up = _dot(xa, wub_ref[0:half, :]) + _dot(xb, wub_ref[half:2 * half, :])
        o_ref[...] = (gt * jax.nn.sigmoid(gt) * up).astype(o_ref.dtype)

    @pl.when(i >= nu_ref[0])
    def _():
        o_ref[...] = jnp.zeros_like(o_ref)


def _expert_down_kernel(te_ref, nu_ref, a_ref, wd_ref, o_ref, wdb_ref):
    i = pl.program_id(1)

    @pl.when(i < nu_ref[0])
    def _():
        @pl.when(_new_expert(te_ref, i))
        def _():
            wdb_ref[...] = wd_ref[0].astype(BF16)

        y = _dot(a_ref[...], wdb_ref[...])
        q = y.shape[1] // 2
        hi = lax.bitcast_convert_type(y[:, :q].astype(BF16).astype(F32), jnp.uint32)
        lo = lax.bitcast_convert_type(y[:, q:].astype(BF16).astype(F32), jnp.uint32)
        o_ref[...] = hi | (lo >> 16)

    @pl.when(i >= nu_ref[0])
    def _():
        o_ref[...] = jnp.zeros_like(o_ref)


def _experts(xs, tile_expert, n_used, wg, wu, wd, *, tm, n_split=EXPERT_SPLIT):
    P = xs.shape[0]
    D, F = wg.shape[1], wg.shape[2]
    n_tiles = P // tm
    fh = F // n_split
    dh = D // n_split
    row = lambda j, i, te, nu: jnp.minimum(i, nu[0] - 1)
    act = pl.pallas_call(
        _expert_up_kernel,
        grid_spec=pltpu.PrefetchScalarGridSpec(
            num_scalar_prefetch=2,
            grid=(n_split, n_tiles),
            in_specs=[pl.BlockSpec((tm, D // 2), lambda j, i, te, nu: (row(j, i, te, nu), 0)),
                      pl.BlockSpec((1, D, fh), lambda j, i, te, nu: (te[i], 0, j)),
                      pl.BlockSpec((1, D, fh), lambda j, i, te, nu: (te[i], 0, j))],
            out_specs=pl.BlockSpec((tm, fh), lambda j, i, te, nu: (i, j)),
            scratch_shapes=[pltpu.VMEM((D, fh), BF16), pltpu.VMEM((D, fh), BF16)],
        ),
        out_shape=jax.ShapeDtypeStruct((P, F), BF16),
        compiler_params=_cparams(("arbitrary", "arbitrary")),
        name="experts_up",
    )(tile_expert, n_used, xs, wg, wu)
    return pl.pallas_call(
        _expert_down_kernel,
        grid_spec=pltpu.PrefetchScalarGridSpec(
            num_scalar_prefetch=2,
            grid=(n_split, n_tiles),
            in_specs=[pl.BlockSpec((tm, F), lambda j, i, te, nu: (row(j, i, te, nu), 0)),
                      pl.BlockSpec((1, F, dh), lambda j, i, te, nu: (te[i], 0, j))],
            out_specs=pl.BlockSpec((tm, dh // 2), lambda j, i, te, nu: (i, j)),
            scratch_shapes=[pltpu.VMEM((F, dh), BF16)],
        ),
        out_shape=jax.ShapeDtypeStruct((P, D // 2), jnp.uint32),
        compiler_params=_cparams(("arbitrary", "arbitrary")),
        name="experts_down",
    )(tile_expert, n_used, act, wd)


def _row_copy(src_ref, row, dst_ref, slot, sem):
    return pltpu.make_async_copy(src_ref.at[row], dst_ref.at[slot], sem)


def _wait_rows(src_ref, dst_ref, sem):
    pltpu.make_async_copy(src_ref.at[pl.ds(0, dst_ref.shape[0])], dst_ref, sem).wait()


def _gather_kernel(ns_ref, idx_ref, src_ref, o_ref, sem):
    i = pl.program_id(0)

    @pl.when(i < ns_ref[0])
    def _():
        rows = o_ref.shape[0]

        def issue(g, carry):
            for u in range(ISSUE_UNROLL):
                r = g * ISSUE_UNROLL + u
                _row_copy(src_ref, idx_ref[r], o_ref, r, sem.at[u % 2]).start(priority=u % 2)
            return carry

        lax.fori_loop(0, rows // ISSUE_UNROLL, issue, 0)
        for prio in range(2):
            _wait_rows(src_ref, o_ref.at[pl.ds(0, rows // 2)], sem.at[prio])

    @pl.when(i >= ns_ref[0])
    def _():
        o_ref[...] = jnp.zeros_like(o_ref)


def _gather_rows(src, idx, n_steps_used, *, tg):
    P = idx.shape[0]
    D = src.shape[1]
    out_dtype = src.dtype
    return pl.pallas_call(
        _gather_kernel,
        grid_spec=pltpu.PrefetchScalarGridSpec(
            num_scalar_prefetch=1,
            grid=(P // tg,),
            in_specs=[pl.BlockSpec((tg,), lambda i, ns: (i,), memory_space=pltpu.SMEM),
                      pl.BlockSpec(memory_space=pl.ANY)],
            out_specs=pl.BlockSpec((tg, D), lambda i, ns: (i, 0)),
            scratch_shapes=[pltpu.SemaphoreType.DMA((2,))],
        ),
        out_shape=jax.ShapeDtypeStruct((P, D), out_dtype),
        compiler_params=pltpu.CompilerParams(dimension_semantics=("arbitrary",), vmem_limit_bytes=VMEM_LIMIT,
                                             disable_bounds_checks=True),
        name="dispatch_gather",
    )(n_steps_used, idx, src)


def _combine_kernel(pos_ref, ys_ref, route_ref, x1_ref, pg_ref, gf_ref, o_ref, y0_ref, y1_ref, sem0, sem1, *,
                    n_split):
    tm = x1_ref.shape[0]

    def issue(g, carry):
        for u in range(ISSUE_UNROLL):
            t = g * ISSUE_UNROLL + u
            _row_copy(ys_ref, pos_ref[2 * t], y0_ref, t, sem0).start(priority=0)
            _row_copy(ys_ref, pos_ref[2 * t + 1], y1_ref, t, sem1).start(priority=1)
        return carry

    lax.fori_loop(0, tm // ISSUE_UNROLL, issue, 0)
    route = route_ref[...]
    w0 = route[:, 2:3]
    w1 = route[:, 3:4]
    _wait_rows(ys_ref, y0_ref, sem0)
    _wait_rows(ys_ref, y1_ref, sem1)
    hi_of = lambda w: lax.bitcast_convert_type(w & jnp.uint32(0xFFFF0000), F32)
    lo_of = lambda w: lax.bitcast_convert_type(w << 16, F32)
    p0 = y0_ref[...]
    p1 = y1_ref[...]
    y_hi = hi_of(p0) * w0 + hi_of(p1) * w1
    y_lo = lo_of(p0) * w0 + lo_of(p1) * w1
    d_model = x1_ref.shape[1]
    ms = (jnp.sum(y_hi * y_hi, axis=-1, keepdims=True) + jnp.sum(y_lo * y_lo, axis=-1, keepdims=True)) / d_model
    scale = lax.rsqrt(ms + NORM_EPS)
    q = d_model // (2 * n_split)
    for j in range(n_split):
        for part, y_part in enumerate((y_hi, y_lo)):
            cols = slice((2 * j + part) * q, (2 * j + part + 1) * q)
            yn = y_part[:, j * q:(j + 1) * q] * scale * pg_ref[:, cols]
            o_ref[:, cols] = x1_ref[:, cols] + gf_ref[0, :, cols] * yn


def _combine(ys, pos, route, x1, post_gain, g_f, *, seq, tm):
    T, D = x1.shape
    B = T // seq
    tpb = seq // tm
    return pl.pallas_call(
        functools.partial(_combine_kernel, n_split=EXPERT_SPLIT),
        grid=(T // tm,),
        in_specs=[pl.BlockSpec((2 * tm,), lambda i: (i,), memory_space=pltpu.SMEM),
                  pl.BlockSpec(memory_space=pl.ANY),
                  pl.BlockSpec((tm, LANES), lambda i: (i, 0)),
                  pl.BlockSpec((tm, D), lambda i: (i, 0)),
                  pl.BlockSpec((1, D), lambda i: (0, 0)),
                  pl.BlockSpec((1, 1, D), lambda i: (i // tpb, 0, 0))],
        out_specs=pl.BlockSpec((tm, D), lambda i: (i, 0)),
        out_shape=jax.ShapeDtypeStruct((T, D), F32),
        scratch_shapes=[pltpu.VMEM((tm, D // 2), ys.dtype), pltpu.VMEM((tm, D // 2), ys.dtype),
                        pltpu.SemaphoreType.DMA(()), pltpu.SemaphoreType.DMA(())],
        compiler_params=pltpu.CompilerParams(dimension_semantics=("arbitrary",), vmem_limit_bytes=VMEM_LIMIT,
                                             disable_bounds_checks=True),
        name="combine",
    )(pos, ys, route, x1, post_gain.reshape(1, D), g_f.reshape(B, 1, D))


def _dispatch_plan(expert_flat, tm, n_tiles):
    n = expert_flat.shape[0]
    onehot = (expert_flat[:, None] == jnp.arange(N_EXPERTS, dtype=jnp.int32)[None, :]).astype(jnp.int32)
    csum = jnp.cumsum(onehot, axis=0)
    counts = csum[-1]
    rank = jnp.sum((csum - onehot) * onehot, axis=1)
    tiles_e = (counts + tm - 1) // tm
    tile_end = jnp.cumsum(tiles_e)
    tile_start = tile_end - tiles_e
    pos = tile_start[expert_flat] * tm + rank
    n_used = tile_end[-1]
    tile_ids = jnp.minimum(jnp.arange(n_tiles, dtype=jnp.int32), n_used - 1)
    tile_expert = jnp.minimum(jnp.searchsorted(tile_end, tile_ids, side="right"), N_EXPERTS - 1).astype(jnp.int32)
    row_token = jnp.zeros((n_tiles * tm,), jnp.int32).at[pos].set(jnp.arange(n, dtype=jnp.int32) // 2)
    return pos, row_token, tile_expert, n_used.astype(jnp.int32).reshape(1)


def _layer(x, c, rel_bias, w_mod, b_mod, attn_pre_gain, attn_post_gain, w_in, shift_mu, w_decay_up, decay_bias,
           w_aicl_up, aicl_bias, w_gate_up, k_k, k_a, r_k, gn_w, gn_b, lam_q1, lam_k1, lam_q2, lam_k2, subln_w,
           w_out, ffn_pre_gain, ffn_post_gain, w_coarse, b_coarse, w_fine, b_fine, w_exp_gate, w_exp_up,
           w_exp_down, *, layer_index, tm_proj, tq, rsub_attn, tm_moe):
    B, S, D = x.shape
    T = B * S
    W = w_decay_up.shape[1]
    rw_cols = shift_mu.shape[0]
    lambda_init = 0.8 - 0.6 * math.exp(-0.3 * layer_index)
    x2 = x.reshape(T, D)

    mod = _mod(c, w_mod, b_mod)
    sh_a, sc_a, g_a, sh_f, sc_f, g_f = jnp.split(mod, 6, axis=-1)

    q_cols = DIFF_HEADS * 2 * DIFF_HEAD_DIM
    q_scale = DIFF_HEAD_DIM ** -0.5 * LOG2E
    w_rw_b = w_in[:, :rw_cols].astype(BF16)
    w_da_b = jnp.concatenate([w_in[:, rw_cols:rw_cols + q_cols] * q_scale, w_in[:, rw_cols + q_cols:]],
                             axis=1).astype(BF16)
    p_rw = _inproj(x2, attn_pre_gain, sc_a, sh_a, w_rw_b, shift_mu,
                   shift=True, out_dtype=F32, seq=S, tm=tm_proj)
    p_da = _inproj(x2, attn_pre_gain, sc_a, sh_a, w_da_b, jnp.zeros((w_in.shape[1] - rw_cols,), F32),
                   shift=False, out_dtype=BF16, seq=S, tm=tm_proj)

    o_rw = _rwkv(p_rw, w_decay_up, decay_bias, w_aicl_up, aicl_bias, w_gate_up, k_k, k_a, r_k.reshape(-1),
                 gn_w, gn_b, seq=S)
    o_da = _diff_attention(p_da, rel_bias, lam_q1, lam_k1, lam_q2, lam_k2, subln_w, lambda_init, seq=S, tq=tq,
                           rsub=rsub_attn)

    w_route = jnp.zeros((D, LANES), F32).at[:, :N_GROUPS].set(w_coarse).at[:, N_GROUPS:N_GROUPS + N_EXPERTS].set(w_fine)
    b_route = jnp.zeros((1, LANES), F32).at[0, :N_GROUPS].set(b_coarse).at[0, N_GROUPS:N_GROUPS + N_EXPERTS].set(b_fine)
    x1, h2, route = _outproj(o_rw, o_da, x2, w_out.astype(BF16), attn_post_gain, g_a, ffn_pre_gain, sc_f, sh_f,
                             w_route, b_route, seq=S, tm=min(2 * tm_proj, S))

    expert_flat = route[:, 0:2].astype(jnp.int32).reshape(-1)
    n_tiles = (2 * T) // tm_moe + N_EXPERTS
    pos, row_token, tile_expert, n_used = _dispatch_plan(expert_flat, tm_moe, n_tiles)
    gather_steps_used = (n_used * tm_moe + GATHER_ROWS - 1) // GATHER_ROWS
    xs = _gather_rows(h2, row_token, gather_steps_used, tg=GATHER_ROWS)
    ys = _experts(xs, tile_expert, n_used, w_exp_gate, w_exp_up, w_exp_down, tm=tm_moe)
    out = _combine(ys, pos, route, x1, ffn_post_gain, g_f, seq=S, tm=COMBINE_ROWS)
    return out.reshape(B, S, D)


def kernel(x, c, rel_bias, w_mod, b_mod, attn_pre_gain, attn_post_gain, w_in, shift_mu, w_decay_up, decay_bias,
           w_aicl_up, aicl_bias, w_gate_up, k_k, k_a, r_k, gn_w, gn_b, lam_q1, lam_k1, lam_q2, lam_k2, subln_w,
           w_out, ffn_pre_gain, ffn_post_gain, w_coarse, b_coarse, w_fine, b_fine, w_exp_gate, w_exp_up,
           w_exp_down):
    depth = w_mod.shape[0]
    S = x.shape[1]
    tm_proj = min(256, S)
    tq = min(1024, S)
    tm_moe = 512 if S >= 4096 else 128
    for l in range(depth):
        x = _layer(x, c, rel_bias, w_mod[l], b_mod[l], attn_pre_gain[l], attn_post_gain[l], w_in[l], shift_mu[l],
                   w_decay_up[l], decay_bias[l], w_aicl_up[l], aicl_bias[l], w_gate_up[l], k_k[l], k_a[l], r_k[l],
                   gn_w[l], gn_b[l], lam_q1[l], lam_k1[l], lam_q2[l], lam_k2[l], subln_w[l], w_out[l],
                   ffn_pre_gain[l], ffn_post_gain[l], w_coarse[l], b_coarse[l], w_fine[l], b_fine[l],
                   w_exp_gate[l], w_exp_up[l], w_exp_down[l], layer_index=l, tm_proj=tm_proj, tq=tq,
                   rsub_attn=min(256, tq), tm_moe=tm_moe)
    return x
```

```python
import functools
import math

import jax
import jax.numpy as jnp
from jax import lax
from jax.experimental import pallas as pl
from jax.experimental.pallas import tpu as pltpu

F32 = jnp.float32
BF16 = jnp.bfloat16

RWKV_HEAD_DIM = 64
DECAY_RANK = 64
AICL_RANK = 64
GATE_RANK = 128
RWKV_GN_EPS = 64e-5
DIFF_HEADS = 8
DIFF_HEAD_DIM = 64
N_BUCKETS = 32
MAX_DISTANCE = 128
SUBLN_EPS = 1e-5
N_GROUPS = 8
EXPERTS_PER_GROUP = 8
N_EXPERTS = N_GROUPS * EXPERTS_PER_GROUP
NORM_EPS = 1e-6
MASK_VALUE = -1e30
LOG2E = math.log2(math.e)

LANES = 128
RWKV_CHUNK = 64
VMEM_LIMIT = 56 * 1024 * 1024
GATHER_ROWS = 1024
COMBINE_ROWS = 512
ISSUE_UNROLL = 8
EXPERT_SPLIT = 2


def _cparams(sem):
    return pltpu.CompilerParams(dimension_semantics=sem, vmem_limit_bytes=VMEM_LIMIT)


def _dot(a, b):
    return jnp.dot(a, b, preferred_element_type=F32)


def _dot_nt(a, b):
    return lax.dot_general(a, b, (((1,), (1,)), ((), ())), preferred_element_type=F32)


def _dot_tn(a, b):
    return lax.dot_general(a, b, (((0,), (0,)), ((), ())), preferred_element_type=F32)


def _mod_kernel(c_ref, w_ref, b_ref, o_ref):
    c = c_ref[...]
    s = c * jax.nn.sigmoid(c)
    o_ref[...] = _dot(s, w_ref[...]) + b_ref[...]


def _mod(c, w_mod, b_mod, tn=1024):
    B, D = c.shape
    N = w_mod.shape[1]
    rows = 8
    c8 = jnp.zeros((rows, D), F32).at[:B].set(c)
    out = pl.pallas_call(
        _mod_kernel,
        grid=(N // tn,),
        in_specs=[pl.BlockSpec((rows, D), lambda j: (0, 0)),
                  pl.BlockSpec((D, tn), lambda j: (0, j)),
                  pl.BlockSpec((1, tn), lambda j: (0, j))],
        out_specs=pl.BlockSpec((rows, tn), lambda j: (0, j)),
        out_shape=jax.ShapeDtypeStruct((rows, N), F32),
        compiler_params=_cparams(("arbitrary",)),
        name="mod",
    )(c8, w_mod, b_mod.reshape(1, N))
    return out[:B]


def _inproj_kernel(x_ref, gain_ref, sc_ref, sh_ref, w_ref, mu_ref, o_ref, carry_ref, *, shift, tiles_per_batch):
    i = pl.program_id(0)
    x = x_ref[...]
    ms = jnp.mean(x * x, axis=-1, keepdims=True)
    h = x * lax.rsqrt(ms + NORM_EPS) * gain_ref[...]
    h = h * (1.0 + sc_ref[0]) + sh_ref[0]
    p = _dot(h.astype(BF16), w_ref[...])
    if shift:
        tm = p.shape[0]

        @pl.when(i % tiles_per_batch == 0)
        def _():
            carry_ref[...] = jnp.zeros_like(carry_ref)

        prev = pltpu.roll(p, 1, axis=0)
        row = lax.broadcasted_iota(jnp.int32, p.shape, 0)
        prev = jnp.where(row == 0, carry_ref[0:1, :], prev)
        carry_ref[0:1, :] = p[tm - 1:tm, :]
        p = p + (prev - p) * mu_ref[...]
    o_ref[...] = p.astype(o_ref.dtype)


def _inproj(x2, gain, sc, sh, w_bf16, mu, *, shift, out_dtype, seq, tm):
    T, D = x2.shape
    N = w_bf16.shape[1]
    B = T // seq
    tiles_per_batch = seq // tm
    kern = functools.partial(_inproj_kernel, shift=shift, tiles_per_batch=tiles_per_batch)
    return pl.pallas_call(
        kern,
        grid=(T // tm,),
        in_specs=[pl.BlockSpec((tm, D), lambda i: (i, 0)),
                  pl.BlockSpec((1, D), lambda i: (0, 0)),
                  pl.BlockSpec((1, 1, D), lambda i: (i // tiles_per_batch, 0, 0)),
                  pl.BlockSpec((1, 1, D), lambda i: (i // tiles_per_batch, 0, 0)),
                  pl.BlockSpec((D, N), lambda i: (0, 0)),
                  pl.BlockSpec((1, N), lambda i: (0, 0))],
        out_specs=pl.BlockSpec((tm, N), lambda i: (i, 0)),
        out_shape=jax.ShapeDtypeStruct((T, N), out_dtype),
        scratch_shapes=[pltpu.VMEM((8, N), F32)],
        compiler_params=_cparams(("arbitrary",)),
        name="inproj_shift" if shift else "inproj",
    )(x2, gain.reshape(1, D), sc.reshape(B, 1, D), sh.reshape(B, 1, D), w_bf16, mu.reshape(1, N))


def _rwkv_kernel(p_ref, wdu_ref, db_ref, wau_ref, ab_ref, wgu_ref, kk_ref, ka_ref, rk_ref,
                 gnw_ref, gnb_ref, o_ref, s_ref, *, width):
    L = RWKV_CHUNK
    W = width
    n_pairs = W // LANES
    nb = p_ref.shape[0]
    c = pl.program_id(0)

    @pl.when(c == 0)
    def _():
        s_ref[...] = jnp.zeros_like(s_ref)

    cols = lambda c0, c1: p_ref[:, :, c0:c1].reshape(nb * L, c1 - c0)
    r = cols(0, W)
    k = cols(W, 2 * W)
    v = cols(2 * W, 3 * W)
    xw = cols(3 * W, 3 * W + DECAY_RANK)
    xa = cols(3 * W + DECAY_RANK, 3 * W + DECAY_RANK + AICL_RANK)
    xg = cols(3 * W + DECAY_RANK + AICL_RANK, 3 * W + DECAY_RANK + AICL_RANK + GATE_RANK)

    d = db_ref[...] + _dot(jnp.tanh(xw).astype(BF16), wdu_ref[...].astype(BF16))
    lw = -math.exp(-0.5) * jax.nn.sigmoid(d)
    a = jax.nn.sigmoid(ab_ref[...] + _dot(xa.astype(BF16), wau_ref[...].astype(BF16)))
    g = _dot(jax.nn.sigmoid(xg).astype(BF16), wgu_ref[...].astype(BF16))
    kkf = k * kk_ref[...]
    kmod = k * (1.0 + (a - 1.0) * ka_ref[...])

    ri = lax.broadcasted_iota(jnp.int32, (nb * L, nb * L), 0)
    ci = lax.broadcasted_iota(jnp.int32, (nb * L, nb * L), 1)
    tri = ((ri >= ci) & (ri // L == ci // L)).astype(BF16)
    lw_hi = lw.astype(BF16)
    rem = lw - lw_hi.astype(F32)
    lw_mid = rem.astype(BF16)
    lw_lo = (rem - lw_mid.astype(F32)).astype(BF16)
    cin = _dot(tri, lw_hi) + _dot(tri, lw_mid) + _dot(tri, lw_lo)
    cex = cin - lw
    c_last = [cin[(b + 1) * L - 1:(b + 1) * L, :] for b in range(nb)]
    e_in = jnp.exp(cin)
    e_ni = jnp.exp(-cin)
    e_ex = jnp.exp(cex)
    e_l = jnp.exp(jnp.concatenate([jnp.broadcast_to(cl, (L, W)) for cl in c_last], axis=0) - cin)
    g_last = [jnp.exp(cl) for cl in c_last]

    lane = lax.broadcasted_iota(jnp.int32, (L, LANES), 1)
    lo = lane < RWKV_HEAD_DIM
    i2 = lax.broadcasted_iota(jnp.int32, (2 * L, LANES), 0)
    j2 = lax.broadcasted_iota(jnp.int32, (2 * L, LANES), 1)
    same_blk = (i2 // L) == (j2 // RWKV_HEAD_DIM)
    mask_strict = same_blk & (i2 > j2)
    mask_incl = same_blk & (i2 >= j2)
    eye = (i2 == j2).astype(F32)

    def stack(t):
        return jnp.concatenate([jnp.where(lo, t, 0.0), jnp.where(lo, 0.0, t)], axis=0)

    def dup(t):
        return jnp.concatenate([t, t], axis=0)

    def headsum(t):
        s_lo = jnp.sum(jnp.where(lo, t, 0.0), axis=-1, keepdims=True)
        s_hi = jnp.sum(jnp.where(lo, 0.0, t), axis=-1, keepdims=True)
        return jnp.where(lo, s_lo, s_hi)

    inv_n = 1.0 / RWKV_HEAD_DIM
    chains = [(b * n_pairs + hp, b, slice(b * L, (b + 1) * L), slice(hp * LANES, (hp + 1) * LANES))
              for b in range(nb) for hp in range(n_pairs)]
    pairs = range(len(chains))
    lh, rh, v_st, bk = [], [], [], []
    for _, _, rb, sl in chains:
        kk_p = kkf[rb, sl]
        nrm = jnp.maximum(jnp.sqrt(headsum(kk_p * kk_p)), 1e-12)
        kn = kk_p / nrm
        bp = kn * a[rb, sl]
        ag = stack(-kn * e_ex[rb, sl])
        rg = stack(r[rb, sl] * e_in[rb, sl])
        bd = dup(bp * e_ni[rb, sl])
        kd = dup(kmod[rb, sl] * e_ni[rb, sl])
        bl = stack(bp * e_l[rb, sl])
        kl = stack(kmod[rb, sl] * e_l[rb, sl])
        v_st.append(stack(v[rb, sl]).astype(BF16))
        lh.append(jnp.concatenate([ag, rg], axis=0).astype(BF16))
        rh.append(jnp.concatenate([bd, kd], axis=0).astype(BF16))
        bk.append(jnp.concatenate([bl, kl], axis=0).astype(BF16))
    a4 = [_dot_nt(lh[i], rh[i]) for i in pairs]
    a_ab = [jnp.where(mask_strict, a4[i][0:2 * L, 0:2 * L], 0.0) for i in pairs]
    a_ak = [jnp.where(mask_strict, a4[i][0:2 * L, 2 * L:4 * L], 0.0).astype(BF16) for i in pairs]
    a_rb = [jnp.where(mask_incl, a4[i][2 * L:4 * L, 0:2 * L], 0.0).astype(BF16) for i in pairs]
    a_rk = [jnp.where(mask_incl, a4[i][2 * L:4 * L, 2 * L:4 * L], 0.0).astype(BF16) for i in pairs]

    pw = a_ab
    tinv = [eye + a_ab[i] for i in pairs]
    for _ in range(int(math.log2(L)) - 1):
        pb = [pw[i].astype(BF16) for i in pairs]
        pw = [_dot(pb[i], pb[i]) for i in pairs]
        tinv = [tinv[i] + _dot(tinv[i].astype(BF16), pw[i].astype(BF16)) for i in pairs]

    s0 = [s_ref[chains[i][0]] for i in pairs]
    xs = [_dot_nt(lh[i], s0[i].astype(BF16)) for i in pairs]
    akv = [_dot(a_ak[i], v_st[i]) for i in pairs]
    u_b = [_dot(tinv[i].astype(BF16), (xs[i][0:2 * L] + akv[i]).astype(BF16)).astype(BF16) for i in pairs]
    y = [xs[i][2 * L:4 * L] + _dot(a_rb[i], u_b[i]) + _dot(a_rk[i], v_st[i]) for i in pairs]
    for i, (si, b, rb, sl) in enumerate(chains):
        uv = jnp.concatenate([u_b[i], v_st[i]], axis=0)
        s_ref[si] = s0[i] * g_last[b][:, sl] + _dot_tn(uv, bk[i])

    for i, (si, b, rb, sl) in enumerate(chains):
        mu = jnp.sum(y[i], axis=-1, keepdims=True) * inv_n
        dlt = jnp.where(same_blk, y[i] - mu, 0.0)
        var = jnp.sum(dlt * dlt, axis=-1, keepdims=True) * inv_n
        yn = dlt * lax.rsqrt(var + RWKV_GN_EPS)
        y_p = (yn[0:L] + yn[L:2 * L]) * gnw_ref[:, sl] + gnb_ref[:, sl]
        bonus = headsum(r[rb, sl] * kmod[rb, sl] * rk_ref[:, sl]) * v[rb, sl]
        o_ref[b, :, sl] = ((y_p + bonus) * g[rb, sl]).astype(o_ref.dtype)


def _rwkv(p_rw, w_decay_up, decay_bias, w_aicl_up, aicl_bias, w_gate_up, k_k, k_a, r_k, gn_w, gn_b, *, seq):
    T, C = p_rw.shape
    W = w_decay_up.shape[1]
    B = T // seq
    L = RWKV_CHUNK
    nC = seq // L
    row = lambda t: t.reshape(1, W)
    full = lambda shape: pl.BlockSpec(shape, lambda c: (0,) * len(shape))
    out = pl.pallas_call(
        functools.partial(_rwkv_kernel, width=W),
        grid=(nC,),
        in_specs=[pl.BlockSpec((B, L, C), lambda c: (0, c, 0)),
                  full((DECAY_RANK, W)), full((1, W)), full((AICL_RANK, W)), full((1, W)),
                  full((GATE_RANK, W)), full((1, W)), full((1, W)), full((1, W)), full((1, W)), full((1, W))],
        out_specs=pl.BlockSpec((B, L, W), lambda c: (0, c, 0)),
        out_shape=jax.ShapeDtypeStruct((B, seq, W), BF16),
        scratch_shapes=[pltpu.VMEM((B * (W // LANES), LANES, LANES), F32)],
        compiler_params=_cparams(("arbitrary",)),
        name="rwkv",
    )(p_rw.reshape(B, seq, C), w_decay_up, row(decay_bias), w_aicl_up, row(aicl_bias), w_gate_up, row(k_k),
      row(k_a), row(r_k), row(gn_w), row(gn_b))
    return out.reshape(T, W)


def _t5_bucket(rel):
    n = jnp.maximum(rel, 0)
    max_exact = N_BUCKETS // 2
    nf = jnp.maximum(n, 1).astype(F32)
    large = max_exact + (jnp.log(nf / max_exact) / math.log(MAX_DISTANCE / max_exact)
                         * (N_BUCKETS - max_exact)).astype(jnp.int32)
    large = jnp.minimum(large, N_BUCKETS - 1)
    return jnp.where(n < max_exact, n, large)


def _attn_kernel(q_ref, k_ref, v_ref, tb_ref, lq1_ref, lk1_ref, lq2_ref, lk2_ref,
                 sw_ref, o_ref, qs_ref, m_ref, l_ref, acc_ref, *, tq, rsub, lambda_init):
    qi = pl.program_id(2)
    tk = tq

    q = q_ref[...]
    lane = lax.broadcasted_iota(jnp.int32, q.shape, 1)
    zero = jnp.zeros_like(q)
    qs_ref[0:tq, :] = jnp.where(lane < DIFF_HEAD_DIM, q, zero)
    qs_ref[tq:2 * tq, :] = jnp.where(lane < DIFF_HEAD_DIM, zero, q)
    m_ref[...] = jnp.full_like(m_ref, MASK_VALUE)
    l_ref[...] = jnp.zeros_like(l_ref)
    acc_ref[...] = jnp.zeros_like(acc_ref)

    band = tb_ref.shape[2]

    def add_bias(s, bias):
        nb = bias.shape[1]
        if nb == s.shape[1]:
            return s + bias
        return jnp.concatenate([s[:, :s.shape[1] - nb], s[:, s.shape[1] - nb:] + bias], axis=1)

    def tile(kj, mode):
        koff = pl.multiple_of(kj * tk, tk)
        subs = [slice(sb * rsub, (sb + 1) * rsub) for sb in range(2 * tq // rsub)]
        r0s = [(sb * rsub) % tq for sb in range(len(subs))]
        widths = [r0 + rsub if mode == "diag" else tk for r0 in r0s]
        m_prev = [m_ref[rows, :] for rows in subs]
        l_prev = [l_ref[rows, :] for rows in subs]
        acc_prev = [acc_ref[rows, :] for rows in subs]
        s_all = [_dot_nt(qs_ref[rows, :], k_ref[pl.ds(koff, w), :]) for rows, w in zip(subs, widths)]
        m_out, l_out, alphas, ps = [], [], [], []
        for sb, s in enumerate(s_all):
            if mode == "diag":
                s = add_bias(s, tb_ref[0] if r0s[sb] > 0 else tb_ref[0, :, band - rsub:])
            elif mode == "near" and r0s[sb] == 0:
                s = add_bias(s, tb_ref[0, :, 0:LANES])
            m_new = jnp.maximum(m_prev[sb], jnp.max(s, axis=-1, keepdims=True))
            alpha = jnp.exp2(m_prev[sb] - m_new)
            p = jnp.exp2(s - jnp.concatenate([m_new] * (widths[sb] // LANES), axis=1))
            m_out.append(m_new)
            l_out.append(alpha * l_prev[sb] + jnp.sum(p, axis=-1, keepdims=True))
            alphas.append(alpha)
            ps.append(p.astype(BF16))
        pv = [_dot(p, v_ref[pl.ds(koff, w), :]) for p, w in zip(ps, widths)]
        for sb, rows in enumerate(subs):
            m_ref[rows, :] = m_out[sb]
            l_ref[rows, :] = l_out[sb]
            acc_ref[rows, :] = alphas[sb] * acc_prev[sb] + pv[sb]

    def far_tile(kj, carry):
        tile(kj, "far")
        return carry

    lax.fori_loop(0, jnp.maximum(qi - 1, 0), far_tile, 0)

    @pl.when(qi >= 1)
    def _():
        tile(qi - 1, "near")

    tile(qi, "diag")
    l = l_ref[...]
    acc = acc_ref[...]
    o1 = acc[0:tq] / l[0:tq]
    o2 = acc[tq:2 * tq] / l[tq:2 * tq]
    lam = (jnp.exp(jnp.sum(lq1_ref[...] * lk1_ref[...], axis=-1, keepdims=True))
           - jnp.exp(jnp.sum(lq2_ref[...] * lk2_ref[...], axis=-1, keepdims=True)) + lambda_init)
    o = o1 - lam * o2
    o = o * lax.rsqrt(jnp.mean(o * o, axis=-1, keepdims=True) + SUBLN_EPS)
    o = o * sw_ref[...] * (1.0 - lambda_init)
    o_ref[...] = o.astype(o_ref.dtype)


def _diff_attention(p_da, rel_bias, lam_q1, lam_k1, lam_q2, lam_k2, subln_w, lambda_init, *, seq, tq, rsub):
    T, C = p_da.shape
    H = DIFF_HEADS
    B = T // seq
    nq = seq // tq
    hd2 = 2 * DIFF_HEAD_DIM
    assert rsub % LANES == 0 and tq % rsub == 0 and MAX_DISTANCE <= LANES
    far = rel_bias[N_BUCKETS - 1]
    band = rsub + LANES
    L = rsub + band
    f = ((rel_bias[_t5_bucket(jnp.arange(L))] - far) * LOG2E).T.astype(F32)
    w = jnp.concatenate([f[:, 0:LANES + 1][:, ::-1], jnp.full((H, band - LANES - 1), MASK_VALUE, F32),
                         f[:, LANES + 1:LANES + rsub + 1][:, ::-1]], axis=1)
    bias_tb = jnp.tile(w, (1, rsub))[:, :rsub * (L - 1)].reshape(H, rsub, L - 1)[:, :, :band]
    vec = lambda t: t.reshape(1, -1)
    small = lambda n: pl.BlockSpec((1, n), lambda b, h, i: (0, 0))
    return pl.pallas_call(
        functools.partial(_attn_kernel, tq=tq, rsub=rsub, lambda_init=lambda_init),
        grid=(B, H, nq),
        in_specs=[pl.BlockSpec((tq, hd2), lambda b, h, i: (b * nq + i, h)),
                  pl.BlockSpec((seq, hd2), lambda b, h, i: (b, H + h)),
                  pl.BlockSpec((seq, hd2), lambda b, h, i: (b, 2 * H + h)),
                  pl.BlockSpec((1, rsub, band), lambda b, h, i: (h, 0, 0)),
                  small(DIFF_HEAD_DIM), small(DIFF_HEAD_DIM), small(DIFF_HEAD_DIM), small(DIFF_HEAD_DIM),
                  small(hd2)],
        out_specs=pl.BlockSpec((tq, hd2), lambda b, h, i: (b * nq + i, h)),
        out_shape=jax.ShapeDtypeStruct((T, H * hd2), BF16),
        scratch_shapes=[pltpu.VMEM((2 * tq, hd2), BF16), pltpu.VMEM((2 * tq, LANES), F32),
                        pltpu.VMEM((2 * tq, LANES), F32), pltpu.VMEM((2 * tq, hd2), F32)],
        compiler_params=_cparams(("arbitrary", "arbitrary", "arbitrary")),
        name="diff_attn",
    )(p_da, p_da, p_da, bias_tb, vec(lam_q1), vec(lam_k1), vec(lam_q2), vec(lam_k2), vec(subln_w))


def _outproj_kernel(orw_ref, oda_ref, x_ref, w1_ref, w2_ref, pg_ref, ga_ref, fg_ref, scf_ref, shf_ref,
                    wr_ref, br_ref, x1_ref, h2_ref, route_ref):
    mix = _dot(orw_ref[...], w1_ref[...]) + _dot(oda_ref[...], w2_ref[...])
    y = mix * lax.rsqrt(jnp.mean(mix * mix, axis=-1, keepdims=True) + NORM_EPS) * pg_ref[...]
    x1 = x_ref[...] + ga_ref[0] * y
    x1_ref[...] = x1
    h2 = x1 * lax.rsqrt(jnp.mean(x1 * x1, axis=-1, keepdims=True) + NORM_EPS) * fg_ref[...]
    h2 = h2 * (1.0 + scf_ref[0]) + shf_ref[0]
    half = h2.shape[1] // 2
    hi = lax.bitcast_convert_type(h2[:, :half].astype(BF16).astype(F32), jnp.uint32)
    lo = lax.bitcast_convert_type(h2[:, half:].astype(BF16).astype(F32), jnp.uint32)
    h2_ref[...] = hi | (lo >> 16)

    logits = jnp.dot(h2, wr_ref[...], preferred_element_type=F32, precision=lax.Precision.HIGHEST) + br_ref[...]
    lane = lax.broadcasted_iota(jnp.int32, logits.shape, 1).astype(F32)
    big = float(LANES)
    neg = -jnp.inf
    cl = jnp.where(lane < N_GROUPS, logits, neg)
    cmax = jnp.max(cl, axis=-1, keepdims=True)
    grp = jnp.min(jnp.where(cl == cmax, lane, big), axis=-1, keepdims=True)
    grp_p = 1.0 / jnp.sum(jnp.exp(cl - cmax), axis=-1, keepdims=True)
    f_lo = N_GROUPS + EXPERTS_PER_GROUP * grp
    fl = jnp.where((lane >= f_lo) & (lane < f_lo + EXPERTS_PER_GROUP), logits, neg)
    v1 = jnp.max(fl, axis=-1, keepdims=True)
    i1 = jnp.min(jnp.where(fl == v1, lane, big), axis=-1, keepdims=True)
    fl2 = jnp.where(lane == i1, neg, fl)
    v2 = jnp.max(fl2, axis=-1, keepdims=True)
    i2 = jnp.min(jnp.where(fl2 == v2, lane, big), axis=-1, keepdims=True)
    e21 = jnp.exp(v2 - v1)
    w1 = grp_p / (1.0 + e21)
    w2 = w1 * e21
    route = jnp.where(lane == 0, i1 - N_GROUPS,
                      jnp.where(lane == 1, i2 - N_GROUPS,
                                jnp.where(lane == 2, w1, jnp.where(lane == 3, w2, 0.0))))
    route_ref[...] = route


def _outproj(o_rw, o_da, x2, w_out_bf16, post_gain, g_a, ffn_gain, sc_f, sh_f, w_route, b_route, *, seq, tm):
    T, D = x2.shape
    W1 = o_rw.shape[1]
    W2 = o_da.shape[1]
    B = T // seq
    tpb = seq // tm
    rowD = lambda: pl.BlockSpec((1, D), lambda i: (0, 0))
    perb = lambda: pl.BlockSpec((1, 1, D), lambda i: (i // tpb, 0, 0))
    r3 = lambda t: t.reshape(B, 1, D)
    return pl.pallas_call(
        _outproj_kernel,
        grid=(T // tm,),
        in_specs=[pl.BlockSpec((tm, W1), lambda i: (i, 0)),
                  pl.BlockSpec((tm, W2), lambda i: (i, 0)),
                  pl.BlockSpec((tm, D), lambda i: (i, 0)),
                  pl.BlockSpec((W1, D), lambda i: (0, 0)),
                  pl.BlockSpec((W2, D), lambda i: (1, 0)),
                  rowD(), perb(), rowD(), perb(), perb(),
                  pl.BlockSpec((D, LANES), lambda i: (0, 0)),
                  pl.BlockSpec((1, LANES), lambda i: (0, 0))],
        out_specs=[pl.BlockSpec((tm, D), lambda i: (i, 0)),
                   pl.BlockSpec((tm, D // 2), lambda i: (i, 0)),
                   pl.BlockSpec((tm, LANES), lambda i: (i, 0))],
        out_shape=[jax.ShapeDtypeStruct((T, D), F32),
                   jax.ShapeDtypeStruct((T, D // 2), jnp.uint32),
                   jax.ShapeDtypeStruct((T, LANES), F32)],
        compiler_params=_cparams(("arbitrary",)),
        name="outproj_router",
    )(o_rw, o_da, x2, w_out_bf16, w_out_bf16, post_gain.reshape(1, D), r3(g_a), ffn_gain.reshape(1, D),
      r3(sc_f), r3(sh_f), w_route, b_route)


def _new_expert(te_ref, i):
    return (i == 0) | (te_ref[i] != te_ref[jnp.maximum(i - 1, 0)])


def _expert_up_kernel(te_ref, nu_ref, x_ref, wg_ref, wu_ref, o_ref, wgb_ref, wub_ref):
    i = pl.program_id(1)

    @pl.when(i < nu_ref[0])
    def _():
        @pl.when(_new_expert(te_ref, i))
        def _():
            wgb_ref[...] = wg_ref[0].astype(BF16)
            wub_ref[...] = wu_ref[0].astype(BF16)

        xw = x_ref[...]
        half = xw.shape[1]
        xa = lax.bitcast_convert_type(xw & jnp.uint32(0xFFFF0000), F32).astype(BF16)
        xb = lax.bitcast_convert_type(xw << 16, F32).astype(BF16)
        gt = _dot(xa, wgb_ref[0:half, :]) + _dot(xb, wgb_ref[half:2 * half, :])
        up = _dot(xa, wub_ref[0:half, :]) + _dot(xb, wub_ref[half:2 * half, :])
        o_ref[...] = (gt * jax.nn.sigmoid(gt) * up).astype(o_ref.dtype)

    @pl.when(i >= nu_ref[0])
    def _():
        o_ref[...] = jnp.zeros_like(o_ref)


def _expert_down_kernel(te_ref, nu_ref, a_ref, wd_ref, o_ref, wdb_ref):
    i = pl.program_id(1)

    @pl.when(i < nu_ref[0])
    def _():
        @pl.when(_new_expert(te_ref, i))
        def _():
            wdb_ref[...] = wd_ref[0].astype(BF16)

        y = _dot(a_ref[...], wdb_ref[...])
        q = y.shape[1] // 2
        hi = lax.bitcast_convert_type(y[:, :q].astype(BF16).astype(F32), jnp.uint32)
        lo = lax.bitcast_convert_type(y[:, q:].astype(BF16).astype(F32), jnp.uint32)
        o_ref[...] = hi | (lo >> 16)

    @pl.when(i >= nu_ref[0])
    def _():
        o_ref[...] = jnp.zeros_like(o_ref)


def _experts(xs, tile_expert, n_used, wg, wu, wd, *, tm, n_split=EXPERT_SPLIT):
    P = xs.shape[0]
    D, F = wg.shape[1], wg.shape[2]
    n_tiles = P // tm
    fh = F // n_split
    dh = D // n_split
    row = lambda j, i, te, nu: jnp.minimum(i, nu[0] - 1)
    act = pl.pallas_call(
        _expert_up_kernel,
        grid_spec=pltpu.PrefetchScalarGridSpec(
            num_scalar_prefetch=2,
            grid=(n_split, n_tiles),
            in_specs=[pl.BlockSpec((tm, D // 2), lambda j, i, te, nu: (row(j, i, te, nu), 0)),
                      pl.BlockSpec((1, D, fh), lambda j, i, te, nu: (te[i], 0, j)),
                      pl.BlockSpec((1, D, fh), lambda j, i, te, nu: (te[i], 0, j))],
            out_specs=pl.BlockSpec((tm, fh), lambda j, i, te, nu: (i, j)),
            scratch_shapes=[pltpu.VMEM((D, fh), BF16), pltpu.VMEM((D, fh), BF16)],
        ),
        out_shape=jax.ShapeDtypeStruct((P, F), BF16),
        compiler_params=_cparams(("arbitrary", "arbitrary")),
        name="experts_up",
    )(tile_expert, n_used, xs, wg, wu)
    return pl.pallas_call(
        _expert_down_kernel,
        grid_spec=pltpu.PrefetchScalarGridSpec(
            num_scalar_prefetch=2,
            grid=(n_split, n_tiles),
            in_specs=[pl.BlockSpec((tm, F), lambda j, i, te, nu: (row(j, i, te, nu), 0)),
                      pl.BlockSpec((1, F, dh), lambda j, i, te, nu: (te[i], 0, j))],
            out_specs=pl.BlockSpec((tm, dh // 2), lambda j, i, te, nu: (i, j)),
            scratch_shapes=[pltpu.VMEM((F, dh), BF16)],
        ),
        out_shape=jax.ShapeDtypeStruct((P, D // 2), jnp.uint32),
        compiler_params=_cparams(("arbitrary", "arbitrary")),
        name="experts_down",
    )(tile_expert, n_used, act, wd)


def _row_copy(src_ref, row, dst_ref, slot, sem):
    return pltpu.make_async_copy(src_ref.at[row], dst_ref.at[slot], sem)


def _wait_rows(src_ref, dst_ref, sem):
    pltpu.make_async_copy(src_ref.at[pl.ds(0, dst_ref.shape[0])], dst_ref, sem).wait()


def _gather_kernel(ns_ref, idx_ref, src_ref, o_ref, sem):
    i = pl.program_id(0)

    @pl.when(i < ns_ref[0])
    def _():
        rows = o_ref.shape[0]

        def issue(g, carry):
            for u in range(ISSUE_UNROLL):
                r = g * ISSUE_UNROLL + u
                _row_copy(src_ref, idx_ref[r], o_ref, r, sem.at[u % 2]).start(priority=u % 2)
            return carry

        lax.fori_loop(0, rows // ISSUE_UNROLL, issue, 0)
        for prio in range(2):
            _wait_rows(src_ref, o_ref.at[pl.ds(0, rows // 2)], sem.at[prio])

    @pl.when(i >= ns_ref[0])
    def _():
        o_ref[...] = jnp.zeros_like(o_ref)


def _gather_rows(src, idx, n_steps_used, *, tg):
    P = idx.shape[0]
    D = src.shape[1]
    out_dtype = src.dtype
    return pl.pallas_call(
        _gather_kernel,
        grid_spec=pltpu.PrefetchScalarGridSpec(
            num_scalar_prefetch=1,
            grid=(P // tg,),
            in_specs=[pl.BlockSpec((tg,), lambda i, ns: (i,), memory_space=pltpu.SMEM),
                      pl.BlockSpec(memory_space=pl.ANY)],
            out_specs=pl.BlockSpec((tg, D), lambda i, ns: (i, 0)),
            scratch_shapes=[pltpu.SemaphoreType.DMA((2,))],
        ),
        out_shape=jax.ShapeDtypeStruct((P, D), out_dtype),
        compiler_params=pltpu.CompilerParams(dimension_semantics=("arbitrary",), vmem_limit_bytes=VMEM_LIMIT,
                                             disable_bounds_checks=True),
        name="dispatch_gather",
    )(n_steps_used, idx, src)


def _combine_kernel(pos_ref, ys_ref, route_ref, x1_ref, pg_ref, gf_ref, o_ref, y0_ref, y1_ref, sem0, sem1, *,
                    n_split):
    tm = x1_ref.shape[0]

    def issue(g, carry):
        for u in range(ISSUE_UNROLL):
            t = g * ISSUE_UNROLL + u
            _row_copy(ys_ref, pos_ref[2 * t], y0_ref, t, sem0).start(priority=0)
            _row_copy(ys_ref, pos_ref[2 * t + 1], y1_ref, t, sem1).start(priority=1)
        return carry

    lax.fori_loop(0, tm // ISSUE_UNROLL, issue, 0)
    route = route_ref[...]
    w0 = route[:, 2:3]
    w1 = route[:, 3:4]
    _wait_rows(ys_ref, y0_ref, sem0)
    _wait_rows(ys_ref, y1_ref, sem1)
    hi_of = lambda w: lax.bitcast_convert_type(w & jnp.uint32(0xFFFF0000), F32)
    lo_of = lambda w: lax.bitcast_convert_type(w << 16, F32)
    p0 = y0_ref[...]
    p1 = y1_ref[...]
    y_hi = hi_of(p0) * w0 + hi_of(p1) * w1
    y_lo = lo_of(p0) * w0 + lo_of(p1) * w1
    d_model = x1_ref.shape[1]
    ms = (jnp.sum(y_hi * y_hi, axis=-1, keepdims=True) + jnp.sum(y_lo * y_lo, axis=-1, keepdims=True)) / d_model
    scale = lax.rsqrt(ms + NORM_EPS)
    q = d_model // (2 * n_split)
    for j in range(n_split):
        for part, y_part in enumerate((y_hi, y_lo)):
            cols = slice((2 * j + part) * q, (2 * j + part + 1) * q)
            yn = y_part[:, j * q:(j + 1) * q] * scale * pg_ref[:, cols]
            o_ref[:, cols] = x1_ref[:, cols] + gf_ref[0, :, cols] * yn


def _combine(ys, pos, route, x1, post_gain, g_f, *, seq, tm):
    T, D = x1.shape
    B = T // seq
    tpb = seq // tm
    return pl.pallas_call(
        functools.partial(_combine_kernel, n_split=EXPERT_SPLIT),
        grid=(T // tm,),
        in_specs=[pl.BlockSpec((2 * tm,), lambda i: (i,), memory_space=pltpu.SMEM),
                  pl.BlockSpec(memory_space=pl.ANY),
                  pl.BlockSpec((tm, LANES), lambda i: (i, 0)),
                  pl.BlockSpec((tm, D), lambda i: (i, 0)),
                  pl.BlockSpec((1, D), lambda i: (0, 0)),
                  pl.BlockSpec((1, 1, D), lambda i: (i // tpb, 0, 0))],
        out_specs=pl.BlockSpec((tm, D), lambda i: (i, 0)),
        out_shape=jax.ShapeDtypeStruct((T, D), F32),
        scratch_shapes=[pltpu.VMEM((tm, D // 2), ys.dtype), pltpu.VMEM((tm, D // 2), ys.dtype),
                        pltpu.SemaphoreType.DMA(()), pltpu.SemaphoreType.DMA(())],
        compiler_params=pltpu.CompilerParams(dimension_semantics=("arbitrary",), vmem_limit_bytes=VMEM_LIMIT,
                                             disable_bounds_checks=True),
        name="combine",
    )(pos, ys, route, x1, post_gain.reshape(1, D), g_f.reshape(B, 1, D))


def _dispatch_plan(expert_flat, tm, n_tiles):
    n = expert_flat.shape[0]
    onehot = (expert_flat[:, None] == jnp.arange(N_EXPERTS, dtype=jnp.int32)[None, :]).astype(jnp.int32)
    csum = jnp.cumsum(onehot, axis=0)
    counts = csum[-1]
    rank = jnp.sum((csum - onehot) * onehot, axis=1)
    tiles_e = (counts + tm - 1) // tm
    tile_end = jnp.cumsum(tiles_e)
    tile_start = tile_end - tiles_e
    pos = tile_start[expert_flat] * tm + rank
    n_used = tile_end[-1]
    tile_ids = jnp.minimum(jnp.arange(n_tiles, dtype=jnp.int32), n_used - 1)
    tile_expert = jnp.minimum(jnp.searchsorted(tile_end, tile_ids, side="right"), N_EXPERTS - 1).astype(jnp.int32)
    filler = jnp.arange(n_tiles * tm, dtype=jnp.int32) % (n // 2)
    row_token = filler.at[pos].set(jnp.arange(n, dtype=jnp.int32) // 2)
    return pos, row_token, tile_expert, n_used.astype(jnp.int32).reshape(1)


def _layer(x, c, rel_bias, w_mod, b_mod, attn_pre_gain, attn_post_gain, w_in, shift_mu, w_decay_up, decay_bias,
           w_aicl_up, aicl_bias, w_gate_up, k_k, k_a, r_k, gn_w, gn_b, lam_q1, lam_k1, lam_q2, lam_k2, subln_w,
           w_out, ffn_pre_gain, ffn_post_gain, w_coarse, b_coarse, w_fine, b_fine, w_exp_gate, w_exp_up,
           w_exp_down, *, layer_index, tm_proj, tq, rsub_attn, tm_moe):
    B, S, D = x.shape
    T = B * S
    W = w_decay_up.shape[1]
    rw_cols = shift_mu.shape[0]
    lambda_init = 0.8 - 0.6 * math.exp(-0.3 * layer_index)
    x2 = x.reshape(T, D)

    mod = _mod(c, w_mod, b_mod)
    sh_a, sc_a, g_a, sh_f, sc_f, g_f = jnp.split(mod, 6, axis=-1)

    q_cols = DIFF_HEADS * 2 * DIFF_HEAD_DIM
    q_scale = DIFF_HEAD_DIM ** -0.5 * LOG2E
    w_rw_b = w_in[:, :rw_cols].astype(BF16)
    w_da_b = jnp.concatenate([w_in[:, rw_cols:rw_cols + q_cols] * q_scale, w_in[:, rw_cols + q_cols:]],
                             axis=1).astype(BF16)
    p_rw = _inproj(x2, attn_pre_gain, sc_a, sh_a, w_rw_b, shift_mu,
                   shift=True, out_dtype=F32, seq=S, tm=tm_proj)
    p_da = _inproj(x2, attn_pre_gain, sc_a, sh_a, w_da_b, jnp.zeros((w_in.shape[1] - rw_cols,), F32),
                   shift=False, out_dtype=BF16, seq=S, tm=tm_proj)

    o_rw = _rwkv(p_rw, w_decay_up, decay_bias, w_aicl_up, aicl_bias, w_gate_up, k_k, k_a, r_k.reshape(-1),
                 gn_w, gn_b, seq=S)
    o_da = _diff_attention(p_da, rel_bias, lam_q1, lam_k1, lam_q2, lam_k2, subln_w, lambda_init, seq=S, tq=tq,
                           rsub=rsub_attn)

    w_route = jnp.zeros((D, LANES), F32).at[:, :N_GROUPS].set(w_coarse).at[:, N_GROUPS:N_GROUPS + N_EXPERTS].set(w_fine)
    b_route = jnp.zeros((1, LANES), F32).at[0, :N_GROUPS].set(b_coarse).at[0, N_GROUPS:N_GROUPS + N_EXPERTS].set(b_fine)
    x1, h2, route = _outproj(o_rw, o_da, x2, w_out.astype(BF16), attn_post_gain, g_a, ffn_pre_gain, sc_f, sh_f,
                             w_route, b_route, seq=S, tm=min(2 * tm_proj, S))

    expert_flat = route[:, 0:2].astype(jnp.int32).reshape(-1)
    n_tiles = (2 * T) // tm_moe + N_EXPERTS
    pos, row_token, tile_expert, n_used = _dispatch_plan(expert_flat, tm_moe, n_tiles)
    gather_steps_used = (n_used * tm_moe + GATHER_ROWS - 1) // GATHER_ROWS
    xs = _gather_rows(h2, row_token, gather_steps_used, tg=GATHER_ROWS)
    ys = _experts(xs, tile_expert, n_used, w_exp_gate, w_exp_up, w_exp_down, tm=tm_moe)
    out = _combine(ys, pos, route, x1, ffn_post_gain, g_f, seq=S, tm=COMBINE_ROWS)
    return out.reshape(B, S, D)


def kernel(x, c, rel_bias, w_mod, b_mod, attn_pre_gain, attn_post_gain, w_in, shift_mu, w_decay_up, decay_bias,
           w_aicl_up, aicl_bias, w_gate_up, k_k, k_a, r_k, gn_w, gn_b, lam_q1, lam_k1, lam_q2, lam_k2, subln_w,
           w_out, ffn_pre_gain, ffn_post_gain, w_coarse, b_coarse, w_fine, b_fine, w_exp_gate, w_exp_up,
           w_exp_down):
    depth = w_mod.shape[0]
    S = x.shape[1]
    tm_proj = min(256, S)
    tq = min(1024, S)
    tm_moe = 256 if S >= 4096 else 128
    for l in range(depth):
        x = _layer(x, c, rel_bias, w_mod[l], b_mod[l], attn_pre_gain[l], attn_post_gain[l], w_in[l], shift_mu[l],
                   w_decay_up[l], decay_bias[l], w_aicl_up[l], aicl_bias[l], w_gate_up[l], k_k[l], k_a[l], r_k[l],
                   gn_w[l], gn_b[l], lam_q1[l], lam_k1[l], lam_q2[l], lam_k2[l], subln_w[l], w_out[l],
                   ffn_pre_gain[l], ffn_post_gain[l], w_coarse[l], b_coarse[l], w_fine[l], b_fine[l],
                   w_exp_gate[l], w_exp_up[l], w_exp_down[l], layer_index=l, tm_proj=tm_proj, tq=tq,
                   rsub_attn=min(256, tq), tm_moe=tm_moe)
    return x
```

```python
import functools
import math

import jax
import jax.numpy as jnp
from jax import lax
from jax.experimental import pallas as pl
from jax.experimental.pallas import tpu as pltpu

F32 = jnp.float32
BF16 = jnp.bfloat16

RWKV_HEAD_DIM = 64
DECAY_RANK = 64
AICL_RANK = 64
GATE_RANK = 128
RWKV_GN_EPS = 64e-5
DIFF_HEADS = 8
DIFF_HEAD_DIM = 64
N_BUCKETS = 32
MAX_DISTANCE = 128
SUBLN_EPS = 1e-5
N_GROUPS = 8
EXPERTS_PER_GROUP = 8
N_EXPERTS = N_GROUPS * EXPERTS_PER_GROUP
NORM_EPS = 1e-6
MASK_VALUE = -1e30
LOG2E = math.log2(math.e)

LANES = 128
RWKV_CHUNK = 64
VMEM_LIMIT = 56 * 1024 * 1024
GATHER_ROWS = 1024
COMBINE_ROWS = 512
ISSUE_UNROLL = 8
EXPERT_SPLIT = 2


def _cparams(sem):
    return pltpu.CompilerParams(dimension_semantics=sem, vmem_limit_bytes=VMEM_LIMIT)


def _dot(a, b):
    return jnp.dot(a, b, preferred_element_type=F32)


def _dot_nt(a, b):
    return lax.dot_general(a, b, (((1,), (1,)), ((), ())), preferred_element_type=F32)


def _dot_tn(a, b):
    return lax.dot_general(a, b, (((0,), (0,)), ((), ())), preferred_element_type=F32)


def _mod_kernel(c_ref, w_ref, b_ref, o_ref):
    c = c_ref[...]
    s = c * jax.nn.sigmoid(c)
    o_ref[...] = _dot(s, w_ref[...]) + b_ref[...]


def _mod(c, w_mod, b_mod, tn=1024):
    B, D = c.shape
    N = w_mod.shape[1]
    rows = 8
    c8 = jnp.zeros((rows, D), F32).at[:B].set(c)
    out = pl.pallas_call(
        _mod_kernel,
        grid=(N // tn,),
        in_specs=[pl.BlockSpec((rows, D), lambda j: (0, 0)),
                  pl.BlockSpec((D, tn), lambda j: (0, j)),
                  pl.BlockSpec((1, tn), lambda j: (0, j))],
        out_specs=pl.BlockSpec((rows, tn), lambda j: (0, j)),
        out_shape=jax.ShapeDtypeStruct((rows, N), F32),
        compiler_params=_cparams(("arbitrary",)),
        name="mod",
    )(c8, w_mod, b_mod.reshape(1, N))
    return out[:B]


def _inproj_kernel(x_ref, gain_ref, sc_ref, sh_ref, w_ref, mu_ref, o_ref, carry_ref, *, shift, tiles_per_batch):
    i = pl.program_id(0)
    x = x_ref[...]
    ms = jnp.mean(x * x, axis=-1, keepdims=True)
    h = x * lax.rsqrt(ms + NORM_EPS) * gain_ref[...]
    h = h * (1.0 + sc_ref[0]) + sh_ref[0]
    p = _dot(h.astype(BF16), w_ref[...])
    if shift:
        tm = p.shape[0]

        @pl.when(i % tiles_per_batch == 0)
        def _():
            carry_ref[...] = jnp.zeros_like(carry_ref)

        prev = pltpu.roll(p, 1, axis=0)
        row = lax.broadcasted_iota(jnp.int32, p.shape, 0)
        prev = jnp.where(row == 0, carry_ref[0:1, :], prev)
        carry_ref[0:1, :] = p[tm - 1:tm, :]
        p = p + (prev - p) * mu_ref[...]
    o_ref[...] = p.astype(o_ref.dtype)


def _inproj(x2, gain, sc, sh, w_bf16, mu, *, shift, out_dtype, seq, tm):
    T, D = x2.shape
    N = w_bf16.shape[1]
    B = T // seq
    tiles_per_batch = seq // tm
    kern = functools.partial(_inproj_kernel, shift=shift, tiles_per_batch=tiles_per_batch)
    return pl.pallas_call(
        kern,
        grid=(T // tm,),
        in_specs=[pl.BlockSpec((tm, D), lambda i: (i, 0)),
                  pl.BlockSpec((1, D), lambda i: (0, 0)),
                  pl.BlockSpec((1, 1, D), lambda i: (i // tiles_per_batch, 0, 0)),
                  pl.BlockSpec((1, 1, D), lambda i: (i // tiles_per_batch, 0, 0)),
                  pl.BlockSpec((D, N), lambda i: (0, 0)),
                  pl.BlockSpec((1, N), lambda i: (0, 0))],
        out_specs=pl.BlockSpec((tm, N), lambda i: (i, 0)),
        out_shape=jax.ShapeDtypeStruct((T, N), out_dtype),
        scratch_shapes=[pltpu.VMEM((8, N), F32)],
        compiler_params=_cparams(("arbitrary",)),
        name="inproj_shift" if shift else "inproj",
    )(x2, gain.reshape(1, D), sc.reshape(B, 1, D), sh.reshape(B, 1, D), w_bf16, mu.reshape(1, N))


def _rwkv_kernel(p_ref, wdu_ref, db_ref, wau_ref, ab_ref, wgu_ref, kk_ref, ka_ref, rk_ref,
                 gnw_ref, gnb_ref, o_ref, s_ref, *, width):
    L = RWKV_CHUNK
    W = width
    n_pairs = W // LANES
    nb = p_ref.shape[0]
    c = pl.program_id(0)

    @pl.when(c == 0)
    def _():
        s_ref[...] = jnp.zeros_like(s_ref)

    cols = lambda c0, c1: p_ref[:, :, c0:c1].reshape(nb * L, c1 - c0)
    r = cols(0, W)
    k = cols(W, 2 * W)
    v = cols(2 * W, 3 * W)
    xw = cols(3 * W, 3 * W + DECAY_RANK)
    xa = cols(3 * W + DECAY_RANK, 3 * W + DECAY_RANK + AICL_RANK)
    xg = cols(3 * W + DECAY_RANK + AICL_RANK, 3 * W + DECAY_RANK + AICL_RANK + GATE_RANK)

    d = db_ref[...] + _dot(jnp.tanh(xw).astype(BF16), wdu_ref[...].astype(BF16))
    lw = -math.exp(-0.5) * jax.nn.sigmoid(d)
    a = jax.nn.sigmoid(ab_ref[...] + _dot(xa.astype(BF16), wau_ref[...].astype(BF16)))
    g = _dot(jax.nn.sigmoid(xg).astype(BF16), wgu_ref[...].astype(BF16))
    kkf = k * kk_ref[...]
    kmod = k * (1.0 + (a - 1.0) * ka_ref[...])

    ri = lax.broadcasted_iota(jnp.int32, (nb * L, nb * L), 0)
    ci = lax.broadcasted_iota(jnp.int32, (nb * L, nb * L), 1)
    tri = ((ri >= ci) & (ri // L == ci // L)).astype(BF16)
    lw_hi = lw.astype(BF16)
    rem = lw - lw_hi.astype(F32)
    lw_mid = rem.astype(BF16)
    lw_lo = (rem - lw_mid.astype(F32)).astype(BF16)
    cin = _dot(tri, lw_hi) + _dot(tri, lw_mid) + _dot(tri, lw_lo)
    cex = cin - lw
    c_last = [cin[(b + 1) * L - 1:(b + 1) * L, :] for b in range(nb)]
    e_in = jnp.exp(cin)
    e_ni = jnp.exp(-cin)
    e_ex = jnp.exp(cex)
    e_l = jnp.exp(jnp.concatenate([jnp.broadcast_to(cl, (L, W)) for cl in c_last], axis=0) - cin)
    g_last = [jnp.exp(cl) for cl in c_last]

    lane = lax.broadcasted_iota(jnp.int32, (L, LANES), 1)
    lo = lane < RWKV_HEAD_DIM
    i2 = lax.broadcasted_iota(jnp.int32, (2 * L, LANES), 0)
    j2 = lax.broadcasted_iota(jnp.int32, (2 * L, LANES), 1)
    same_blk = (i2 // L) == (j2 // RWKV_HEAD_DIM)
    mask_strict = same_blk & (i2 > j2)
    mask_incl = same_blk & (i2 >= j2)
    eye = (i2 == j2).astype(F32)

    def stack(t):
        return jnp.concatenate([jnp.where(lo, t, 0.0), jnp.where(lo, 0.0, t)], axis=0)

    def dup(t):
        return jnp.concatenate([t, t], axis=0)

    def headsum(t):
        s_lo = jnp.sum(jnp.where(lo, t, 0.0), axis=-1, keepdims=True)
        s_hi = jnp.sum(jnp.where(lo, 0.0, t), axis=-1, keepdims=True)
        return jnp.where(lo, s_lo, s_hi)

    inv_n = 1.0 / RWKV_HEAD_DIM
    chains = [(b * n_pairs + hp, b, slice(b * L, (b + 1) * L), slice(hp * LANES, (hp + 1) * LANES))
              for b in range(nb) for hp in range(n_pairs)]
    pairs = range(len(chains))
    lh, rh, v_st, bk = [], [], [], []
    for _, _, rb, sl in chains:
        kk_p = kkf[rb, sl]
        nrm = jnp.maximum(jnp.sqrt(headsum(kk_p * kk_p)), 1e-12)
        kn = kk_p / nrm
        bp = kn * a[rb, sl]
        ag = stack(-kn * e_ex[rb, sl])
        rg = stack(r[rb, sl] * e_in[rb, sl])
        bd = dup(bp * e_ni[rb, sl])
        kd = dup(kmod[rb, sl] * e_ni[rb, sl])
        bl = stack(bp * e_l[rb, sl])
        kl = stack(kmod[rb, sl] * e_l[rb, sl])
        v_st.append(stack(v[rb, sl]).astype(BF16))
        lh.append(jnp.concatenate([ag, rg], axis=0).astype(BF16))
        rh.append(jnp.concatenate([bd, kd], axis=0).astype(BF16))
        bk.append(jnp.concatenate([bl, kl], axis=0).astype(BF16))
    a4 = [_dot_nt(lh[i], rh[i]) for i in pairs]
    a_ab = [jnp.where(mask_strict, a4[i][0:2 * L, 0:2 * L], 0.0) for i in pairs]
    a_ak = [jnp.where(mask_strict, a4[i][0:2 * L, 2 * L:4 * L], 0.0).astype(BF16) for i in pairs]
    a_rb = [jnp.where(mask_incl, a4[i][2 * L:4 * L, 0:2 * L], 0.0).astype(BF16) for i in pairs]
    a_rk = [jnp.where(mask_incl, a4[i][2 * L:4 * L, 2 * L:4 * L], 0.0).astype(BF16) for i in pairs]

    pw = a_ab
    tinv = [eye + a_ab[i] for i in pairs]
    for _ in range(int(math.log2(L)) - 1):
        pb = [pw[i].astype(BF16) for i in pairs]
        pw = [_dot(pb[i], pb[i]) for i in pairs]
        tinv = [tinv[i] + _dot(tinv[i].astype(BF16), pw[i].astype(BF16)) for i in pairs]

    s0 = [s_ref[chains[i][0]] for i in pairs]
    xs = [_dot_nt(lh[i], s0[i].astype(BF16)) for i in pairs]
    akv = [_dot(a_ak[i], v_st[i]) for i in pairs]
    u_b = [_dot(tinv[i].astype(BF16), (xs[i][0:2 * L] + akv[i]).astype(BF16)).astype(BF16) for i in pairs]
    y = [xs[i][2 * L:4 * L] + _dot(a_rb[i], u_b[i]) + _dot(a_rk[i], v_st[i]) for i in pairs]
    for i, (si, b, rb, sl) in enumerate(chains):
        uv = jnp.concatenate([u_b[i], v_st[i]], axis=0)
        s_ref[si] = s0[i] * g_last[b][:, sl] + _dot_tn(uv, bk[i])

    for i, (si, b, rb, sl) in enumerate(chains):
        mu = jnp.sum(y[i], axis=-1, keepdims=True) * inv_n
        dlt = jnp.where(same_blk, y[i] - mu, 0.0)
        var = jnp.sum(dlt * dlt, axis=-1, keepdims=True) * inv_n
        yn = dlt * lax.rsqrt(var + RWKV_GN_EPS)
        y_p = (yn[0:L] + yn[L:2 * L]) * gnw_ref[:, sl] + gnb_ref[:, sl]
        bonus = headsum(r[rb, sl] * kmod[rb, sl] * rk_ref[:, sl]) * v[rb, sl]
        o_ref[b, :, sl] = ((y_p + bonus) * g[rb, sl]).astype(o_ref.dtype)


def _rwkv(p_rw, w_decay_up, decay_bias, w_aicl_up, aicl_bias, w_gate_up, k_k, k_a, r_k, gn_w, gn_b, *, seq):
    T, C = p_rw.shape
    W = w_decay_up.shape[1]
    B = T // seq
    L = RWKV_CHUNK
    nC = seq // L
    row = lambda t: t.reshape(1, W)
    full = lambda shape: pl.BlockSpec(shape, lambda c: (0,) * len(shape))
    out = pl.pallas_call(
        functools.partial(_rwkv_kernel, width=W),
        grid=(nC,),
        in_specs=[pl.BlockSpec((B, L, C), lambda c: (0, c, 0)),
                  full((DECAY_RANK, W)), full((1, W)), full((AICL_RANK, W)), full((1, W)),
                  full((GATE_RANK, W)), full((1, W)), full((1, W)), full((1, W)), full((1, W)), full((1, W))],
        out_specs=pl.BlockSpec((B, L, W), lambda c: (0, c, 0)),
        out_shape=jax.ShapeDtypeStruct((B, seq, W), BF16),
        scratch_shapes=[pltpu.VMEM((B * (W // LANES), LANES, LANES), F32)],
        compiler_params=_cparams(("arbitrary",)),
        name="rwkv",
    )(p_rw.reshape(B, seq, C), w_decay_up, row(decay_bias), w_aicl_up, row(aicl_bias), w_gate_up, row(k_k),
      row(k_a), row(r_k), row(gn_w), row(gn_b))
    return out.reshape(T, W)


def _t5_bucket(rel):
    n = jnp.maximum(rel, 0)
    max_exact = N_BUCKETS // 2
    nf = jnp.maximum(n, 1).astype(F32)
    large = max_exact + (jnp.log(nf / max_exact) / math.log(MAX_DISTANCE / max_exact)
                         * (N_BUCKETS - max_exact)).astype(jnp.int32)
    large = jnp.minimum(large, N_BUCKETS - 1)
    return jnp.where(n < max_exact, n, large)


def _attn_kernel(q_ref, k_ref, v_ref, tb_ref, lq1_ref, lk1_ref, lq2_ref, lk2_ref,
                 sw_ref, o_ref, qs_ref, m_ref, l_ref, acc_ref, *, tq, rsub, lambda_init):
    qi = pl.program_id(2)
    tk = tq

    q = q_ref[...]
    lane = lax.broadcasted_iota(jnp.int32, q.shape, 1)
    zero = jnp.zeros_like(q)
    qs_ref[0:tq, :] = jnp.where(lane < DIFF_HEAD_DIM, q, zero)
    qs_ref[tq:2 * tq, :] = jnp.where(lane < DIFF_HEAD_DIM, zero, q)
    m_ref[...] = jnp.full_like(m_ref, MASK_VALUE)
    l_ref[...] = jnp.zeros_like(l_ref)
    acc_ref[...] = jnp.zeros_like(acc_ref)

    band = tb_ref.shape[2]

    def add_bias(s, bias):
        nb = bias.shape[1]
        if nb == s.shape[1]:
            return s + bias
        return jnp.concatenate([s[:, :s.shape[1] - nb], s[:, s.shape[1] - nb:] + bias], axis=1)

    def tile(kj, mode):
        koff = pl.multiple_of(kj * tk, tk)
        subs = [slice(sb * rsub, (sb + 1) * rsub) for sb in range(2 * tq // rsub)]
        r0s = [(sb * rsub) % tq for sb in range(len(subs))]
        widths = [r0 + rsub if mode == "diag" else tk for r0 in r0s]
        m_prev = [m_ref[rows, :] for rows in subs]
        l_prev = [l_ref[rows, :] for rows in subs]
        acc_prev = [acc_ref[rows, :] for rows in subs]
        s_all = [_dot_nt(qs_ref[rows, :], k_ref[pl.ds(koff, w), :]) for rows, w in zip(subs, widths)]
        m_out, l_out, alphas, ps = [], [], [], []
        for sb, s in enumerate(s_all):
            if mode == "diag":
                s = add_bias(s, tb_ref[0] if r0s[sb] > 0 else tb_ref[0, :, band - rsub:])
            elif mode == "near" and r0s[sb] == 0:
                s = add_bias(s, tb_ref[0, :, 0:LANES])
            m_new = jnp.maximum(m_prev[sb], jnp.max(s, axis=-1, keepdims=True))
            alpha = jnp.exp2(m_prev[sb] - m_new)
            p = jnp.exp2(s - jnp.concatenate([m_new] * (widths[sb] // LANES), axis=1))
            m_out.append(m_new)
            l_out.append(alpha * l_prev[sb] + jnp.sum(p, axis=-1, keepdims=True))
            alphas.append(alpha)
            ps.append(p.astype(BF16))
        pv = [_dot(p, v_ref[pl.ds(koff, w), :]) for p, w in zip(ps, widths)]
        for sb, rows in enumerate(subs):
            m_ref[rows, :] = m_out[sb]
            l_ref[rows, :] = l_out[sb]
            acc_ref[rows, :] = alphas[sb] * acc_prev[sb] + pv[sb]

    def far_tile(kj, carry):
        tile(kj, "far")
        return carry

    lax.fori_loop(0, jnp.maximum(qi - 1, 0), far_tile, 0)

    @pl.when(qi >= 1)
    def _():
        tile(qi - 1, "near")

    tile(qi, "diag")
    l = l_ref[...]
    acc = acc_ref[...]
    o1 = acc[0:tq] / l[0:tq]
    o2 = acc[tq:2 * tq] / l[tq:2 * tq]
    lam = (jnp.exp(jnp.sum(lq1_ref[...] * lk1_ref[...], axis=-1, keepdims=True))
           - jnp.exp(jnp.sum(lq2_ref[...] * lk2_ref[...], axis=-1, keepdims=True)) + lambda_init)
    o = o1 - lam * o2
    o = o * lax.rsqrt(jnp.mean(o * o, axis=-1, keepdims=True) + SUBLN_EPS)
    o = o * sw_ref[...] * (1.0 - lambda_init)
    o_ref[...] = o.astype(o_ref.dtype)


def _diff_attention(p_da, rel_bias, lam_q1, lam_k1, lam_q2, lam_k2, subln_w, lambda_init, *, seq, tq, rsub):
    T, C = p_da.shape
    H = DIFF_HEADS
    B = T // seq
    nq = seq // tq
    hd2 = 2 * DIFF_HEAD_DIM
    assert rsub % LANES == 0 and tq % rsub == 0 and MAX_DISTANCE <= LANES
    far = rel_bias[N_BUCKETS - 1]
    band = rsub + LANES
    L = rsub + band
    f = ((rel_bias[_t5_bucket(jnp.arange(L))] - far) * LOG2E).T.astype(F32)
    w = jnp.concatenate([f[:, 0:LANES + 1][:, ::-1], jnp.full((H, band - LANES - 1), MASK_VALUE, F32),
                         f[:, LANES + 1:LANES + rsub + 1][:, ::-1]], axis=1)
    bias_tb = jnp.tile(w, (1, rsub))[:, :rsub * (L - 1)].reshape(H, rsub, L - 1)[:, :, :band]
    vec = lambda t: t.reshape(1, -1)
    small = lambda n: pl.BlockSpec((1, n), lambda b, h, i: (0, 0))
    return pl.pallas_call(
        functools.partial(_attn_kernel, tq=tq, rsub=rsub, lambda_init=lambda_init),
        grid=(B, H, nq),
        in_specs=[pl.BlockSpec((tq, hd2), lambda b, h, i: (b * nq + i, h)),
                  pl.BlockSpec((seq, hd2), lambda b, h, i: (b, H + h)),
                  pl.BlockSpec((seq, hd2), lambda b, h, i: (b, 2 * H + h)),
                  pl.BlockSpec((1, rsub, band), lambda b, h, i: (h, 0, 0)),
                  small(DIFF_HEAD_DIM), small(DIFF_HEAD_DIM), small(DIFF_HEAD_DIM), small(DIFF_HEAD_DIM),
                  small(hd2)],
        out_specs=pl.BlockSpec((tq, hd2), lambda b, h, i: (b * nq + i, h)),
        out_shape=jax.ShapeDtypeStruct((T, H * hd2), BF16),
        scratch_shapes=[pltpu.VMEM((2 * tq, hd2), BF16), pltpu.VMEM((2 * tq, LANES), F32),
                        pltpu.VMEM((2 * tq, LANES), F32), pltpu.VMEM((2 * tq, hd2), F32)],
        compiler_params=_cparams(("arbitrary", "arbitrary", "arbitrary")),
        name="diff_attn",
    )(p_da, p_da, p_da, bias_tb, vec(lam_q1), vec(lam_k1), vec(lam_q2), vec(lam_k2), vec(subln_w))


def _outproj_kernel(orw_ref, oda_ref, x_ref, w1_ref, w2_ref, pg_ref, ga_ref, fg_ref, scf_ref, shf_ref,
                    wr_ref, br_ref, x1_ref, h2_ref, route_ref):
    mix = _dot(orw_ref[...], w1_ref[...]) + _dot(oda_ref[...], w2_ref[...])
    y = mix * lax.rsqrt(jnp.mean(mix * mix, axis=-1, keepdims=True) + NORM_EPS) * pg_ref[...]
    x1 = x_ref[...] + ga_ref[0] * y
    x1_ref[...] = x1
    h2 = x1 * lax.rsqrt(jnp.mean(x1 * x1, axis=-1, keepdims=True) + NORM_EPS) * fg_ref[...]
    h2 = h2 * (1.0 + scf_ref[0]) + shf_ref[0]
    half = h2.shape[1] // 2
    hi = lax.bitcast_convert_type(h2[:, :half].astype(BF16).astype(F32), jnp.uint32)
    lo = lax.bitcast_convert_type(h2[:, half:].astype(BF16).astype(F32), jnp.uint32)
    h2_ref[...] = hi | (lo >> 16)

    logits = jnp.dot(h2, wr_ref[...], preferred_element_type=F32, precision=lax.Precision.HIGHEST) + br_ref[...]
    lane = lax.broadcasted_iota(jnp.int32, logits.shape, 1).astype(F32)
    big = float(LANES)
    neg = -jnp.inf
    cl = jnp.where(lane < N_GROUPS, logits, neg)
    cmax = jnp.max(cl, axis=-1, keepdims=True)
    grp = jnp.min(jnp.where(cl == cmax, lane, big), axis=-1, keepdims=True)
    grp_p = 1.0 / jnp.sum(jnp.exp(cl - cmax), axis=-1, keepdims=True)
    f_lo = N_GROUPS + EXPERTS_PER_GROUP * grp
    fl = jnp.where((lane >= f_lo) & (lane < f_lo + EXPERTS_PER_GROUP), logits, neg)
    v1 = jnp.max(fl, axis=-1, keepdims=True)
    i1 = jnp.min(jnp.where(fl == v1, lane, big), axis=-1, keepdims=True)
    fl2 = jnp.where(lane == i1, neg, fl)
    v2 = jnp.max(fl2, axis=-1, keepdims=True)
    i2 = jnp.min(jnp.where(fl2 == v2, lane, big), axis=-1, keepdims=True)
    e21 = jnp.exp(v2 - v1)
    w1 = grp_p / (1.0 + e21)
    w2 = w1 * e21
    route = jnp.where(lane == 0, i1 - N_GROUPS,
                      jnp.where(lane == 1, i2 - N_GROUPS,
                                jnp.where(lane == 2, w1, jnp.where(lane == 3, w2, 0.0))))
    route_ref[...] = route


def _outproj(o_rw, o_da, x2, w_out_bf16, post_gain, g_a, ffn_gain, sc_f, sh_f, w_route, b_route, *, seq, tm):
    T, D = x2.shape
    W1 = o_rw.shape[1]
    W2 = o_da.shape[1]
    B = T // seq
    tpb = seq // tm
    rowD = lambda: pl.BlockSpec((1, D), lambda i: (0, 0))
    perb = lambda: pl.BlockSpec((1, 1, D), lambda i: (i // tpb, 0, 0))
    r3 = lambda t: t.reshape(B, 1, D)
    return pl.pallas_call(
        _outproj_kernel,
        grid=(T // tm,),
        in_specs=[pl.BlockSpec((tm, W1), lambda i: (i, 0)),
                  pl.BlockSpec((tm, W2), lambda i: (i, 0)),
                  pl.BlockSpec((tm, D), lambda i: (i, 0)),
                  pl.BlockSpec((W1, D), lambda i: (0, 0)),
                  pl.BlockSpec((W2, D), lambda i: (1, 0)),
                  rowD(), perb(), rowD(), perb(), perb(),
                  pl.BlockSpec((D, LANES), lambda i: (0, 0)),
                  pl.BlockSpec((1, LANES), lambda i: (0, 0))],
        out_specs=[pl.BlockSpec((tm, D), lambda i: (i, 0)),
                   pl.BlockSpec((tm, D // 2), lambda i: (i, 0)),
                   pl.BlockSpec((tm, LANES), lambda i: (i, 0))],
        out_shape=[jax.ShapeDtypeStruct((T, D), F32),
                   jax.ShapeDtypeStruct((T, D // 2), jnp.uint32),
                   jax.ShapeDtypeStruct((T, LANES), F32)],
        compiler_params=_cparams(("arbitrary",)),
        name="outproj_router",
    )(o_rw, o_da, x2, w_out_bf16, w_out_bf16, post_gain.reshape(1, D), r3(g_a), ffn_gain.reshape(1, D),
      r3(sc_f), r3(sh_f), w_route, b_route)


def _new_expert(te_ref, i):
    return (i == 0) | (te_ref[i] != te_ref[jnp.maximum(i - 1, 0)])


def _expert_up_kernel(te_ref, nu_ref, x_ref, wg_ref, wu_ref, o_ref, wgb_ref, wub_ref):
    i = pl.program_id(1)

    @pl.when(i < nu_ref[0])
    def _():
        @pl.when(_new_expert(te_ref, i))
        def _():
            wgb_ref[...] = wg_ref[0].astype(BF16)
            wub_ref[...] = wu_ref[0].astype(BF16)

        xw = x_ref[...]
        half = xw.shape[1]
        xa = lax.bitcast_convert_type(xw & jnp.uint32(0xFFFF0000), F32).astype(BF16)
        xb = lax.bitcast_convert_type(xw << 16, F32).astype(BF16)
        gt = _dot(xa, wgb_ref[0:half, :]) + _dot(xb, wgb_ref[half:2 * half, :])
        up = _dot(xa, wub_ref[0:half, :]) + _dot(xb, wub_ref[half:2 * half, :])
        o_ref[...] = (gt * jax.nn.sigmoid(gt) * up).astype(o_ref.dtype)

    @pl.when(i >= nu_ref[0])
    def _():
        o_ref[...] = jnp.zeros_like(o_ref)


def _expert_down_kernel(te_ref, nu_ref, a_ref, wd_ref, o_ref, wdb_ref):
    i = pl.program_id(1)

    @pl.when(i < nu_ref[0])
    def _():
        @pl.when(_new_expert(te_ref, i))
        def _():
            wdb_ref[...] = wd_ref[0].astype(BF16)

        y = _dot(a_ref[...], wdb_ref[...])
        q = y.shape[1] // 2
        hi = lax.bitcast_convert_type(y[:, :q].astype(BF16).astype(F32), jnp.uint32)
        lo = lax.bitcast_convert_type(y[:, q:].astype(BF16).astype(F32), jnp.uint32)
        o_ref[...] = hi | (lo >> 16)

    @pl.when(i >= nu_ref[0])
    def _():
        o_ref[...] = jnp.zeros_like(o_ref)


def _experts(xs, tile_expert, n_used, wg, wu, wd, *, tm, n_split=EXPERT_SPLIT):
    P = xs.shape[0]
    D, F = wg.shape[1], wg.shape[2]
    n_tiles = P // tm
    fh = F // n_split
    dh = D // n_split
    row = lambda j, i, te, nu: jnp.minimum(i, nu[0] - 1)
    act = pl.pallas_call(
        _expert_up_kernel,
        grid_spec=pltpu.PrefetchScalarGridSpec(
            num_scalar_prefetch=2,
            grid=(n_split, n_tiles),
            in_specs=[pl.BlockSpec((tm, D // 2), lambda j, i, te, nu: (row(j, i, te, nu), 0)),
                      pl.BlockSpec((1, D, fh), lambda j, i, te, nu: (te[i], 0, j)),
                      pl.BlockSpec((1, D, fh), lambda j, i, te, nu: (te[i], 0, j))],
            out_specs=pl.BlockSpec((tm, fh), lambda j, i, te, nu: (i, j)),
            scratch_shapes=[pltpu.VMEM((D, fh), BF16), pltpu.VMEM((D, fh), BF16)],
        ),
        out_shape=jax.ShapeDtypeStruct((P, F), BF16),
        compiler_params=_cparams(("arbitrary", "arbitrary")),
        name="experts_up",
    )(tile_expert, n_used, xs, wg, wu)
    return pl.pallas_call(
        _expert_down_kernel,
        grid_spec=pltpu.PrefetchScalarGridSpec(
            num_scalar_prefetch=2,
            grid=(n_split, n_tiles),
            in_specs=[pl.BlockSpec((tm, F), lambda j, i, te, nu: (row(j, i, te, nu), 0)),
                      pl.BlockSpec((1, F, dh), lambda j, i, te, nu: (te[i], 0, j))],
            out_specs=pl.BlockSpec((tm, dh // 2), lambda j, i, te, nu: (i, j)),
            scratch_shapes=[pltpu.VMEM((F, dh), BF16)],
        ),
        out_shape=jax.ShapeDtypeStruct((P, D // 2), jnp.uint32),
        compiler_params=_cparams(("arbitrary", "arbitrary")),
        name="experts_down",
    )(tile_expert, n_used, act, wd)


def _row_copy(src_ref, row, dst_ref, slot, sem):
    return pltpu.make_async_copy(src_ref.at[row], dst_ref.at[slot], sem)


def _wait_rows(src_ref, dst_ref, sem):
    pltpu.make_async_copy(src_ref.at[pl.ds(0, dst_ref.shape[0])], dst_ref, sem).wait()


def _gather_kernel(ns_ref, idx_ref, src_ref, o_ref, sem):
    i = pl.program_id(0)

    @pl.when(i < ns_ref[0])
    def _():
        rows = o_ref.shape[0]

        def issue(g, carry):
            for u in range(ISSUE_UNROLL):
                r = g * ISSUE_UNROLL + u
                _row_copy(src_ref, idx_ref[r], o_ref, r, sem.at[u % 2]).start(priority=u % 2)
            return carry

        lax.fori_loop(0, rows // ISSUE_UNROLL, issue, 0)
        for prio in range(2):
            _wait_rows(src_ref, o_ref.at[pl.ds(0, rows // 2)], sem.at[prio])

    @pl.when(i >= ns_ref[0])
    def _():
        o_ref[...] = jnp.zeros_like(o_ref)


def _gather_rows(src, idx, n_steps_used, *, tg):
    P = idx.shape[0]
    D = src.shape[1]
    out_dtype = src.dtype
    return pl.pallas_call(
        _gather_kernel,
        grid_spec=pltpu.PrefetchScalarGridSpec(
            num_scalar_prefetch=1,
            grid=(P // tg,),
            in_specs=[pl.BlockSpec((tg,), lambda i, ns: (i,), memory_space=pltpu.SMEM),
                      pl.BlockSpec(memory_space=pl.ANY)],
            out_specs=pl.BlockSpec((tg, D), lambda i, ns: (i, 0)),
            scratch_shapes=[pltpu.SemaphoreType.DMA((2,))],
        ),
        out_shape=jax.ShapeDtypeStruct((P, D), out_dtype),
        compiler_params=pltpu.CompilerParams(dimension_semantics=("arbitrary",), vmem_limit_bytes=VMEM_LIMIT,
                                             disable_bounds_checks=True),
        name="dispatch_gather",
    )(n_steps_used, idx, src)


def _combine_kernel(pos_ref, ys_ref, route_ref, x1_ref, pg_ref, gf_ref, o_ref, y0_ref, y1_ref, sem0, sem1, *,
                    n_split):
    tm = x1_ref.shape[0]

    def issue(g, carry):
        for u in range(ISSUE_UNROLL):
            t = g * ISSUE_UNROLL + u
            _row_copy(ys_ref, pos_ref[2 * t], y0_ref, t, sem0).start(priority=0)
            _row_copy(ys_ref, pos_ref[2 * t + 1], y1_ref, t, sem1).start(priority=1)
        return carry

    lax.fori_loop(0, tm // ISSUE_UNROLL, issue, 0)
    route = route_ref[...]
    w0 = route[:, 2:3]
    w1 = route[:, 3:4]
    _wait_rows(ys_ref, y0_ref, sem0)
    _wait_rows(ys_ref, y1_ref, sem1)
    hi_of = lambda w: lax.bitcast_convert_type(w & jnp.uint32(0xFFFF0000), F32)
    lo_of = lambda w: lax.bitcast_convert_type(w << 16, F32)
    p0 = y0_ref[...]
    p1 = y1_ref[...]
    y_hi = hi_of(p0) * w0 + hi_of(p1) * w1
    y_lo = lo_of(p0) * w0 + lo_of(p1) * w1
    d_model = x1_ref.shape[1]
    ms = (jnp.sum(y_hi * y_hi, axis=-1, keepdims=True) + jnp.sum(y_lo * y_lo, axis=-1, keepdims=True)) / d_model
    scale = lax.rsqrt(ms + NORM_EPS)
    q = d_model // (2 * n_split)
    for j in range(n_split):
        for part, y_part in enumerate((y_hi, y_lo)):
            cols = slice((2 * j + part) * q, (2 * j + part + 1) * q)
            yn = y_part[:, j * q:(j + 1) * q] * scale * pg_ref[:, cols]
            o_ref[:, cols] = x1_ref[:, cols] + gf_ref[0, :, cols] * yn


def _combine(ys, pos, route, x1, post_gain, g_f, *, seq, tm):
    T, D = x1.shape
    B = T // seq
    tpb = seq // tm
    return pl.pallas_call(
        functools.partial(_combine_kernel, n_split=EXPERT_SPLIT),
        grid=(T // tm,),
        in_specs=[pl.BlockSpec((2 * tm,), lambda i: (i,), memory_space=pltpu.SMEM),
                  pl.BlockSpec(memory_space=pl.ANY),
                  pl.BlockSpec((tm, LANES), lambda i: (i, 0)),
                  pl.BlockSpec((tm, D), lambda i: (i, 0)),
                  pl.BlockSpec((1, D), lambda i: (0, 0)),
                  pl.BlockSpec((1, 1, D), lambda i: (i // tpb, 0, 0))],
        out_specs=pl.BlockSpec((tm, D), lambda i: (i, 0)),
        out_shape=jax.ShapeDtypeStruct((T, D), F32),
        scratch_shapes=[pltpu.VMEM((tm, D // 2), ys.dtype), pltpu.VMEM((tm, D // 2), ys.dtype),
                        pltpu.SemaphoreType.DMA(()), pltpu.SemaphoreType.DMA(())],
        compiler_params=pltpu.CompilerParams(dimension_semantics=("arbitrary",), vmem_limit_bytes=VMEM_LIMIT,
                                             disable_bounds_checks=True),
        name="combine",
    )(pos, ys, route, x1, post_gain.reshape(1, D), g_f.reshape(B, 1, D))


def _dispatch_plan(expert_flat, tm, n_tiles):
    n = expert_flat.shape[0]
    onehot = (expert_flat[:, None] == jnp.arange(N_EXPERTS, dtype=jnp.int32)[None, :]).astype(jnp.int32)
    csum = jnp.cumsum(onehot, axis=0)
    counts = csum[-1]
    rank = jnp.sum((csum - onehot) * onehot, axis=1)
    tiles_e = (counts + tm - 1) // tm
    tile_end = jnp.cumsum(tiles_e)
    tile_start = tile_end - tiles_e
    pos = tile_start[expert_flat] * tm + rank
    n_used = tile_end[-1]
    tile_ids = jnp.minimum(jnp.arange(n_tiles, dtype=jnp.int32), n_used - 1)
    tile_expert = jnp.sum((tile_end[None, :] <= tile_ids[:, None]).astype(jnp.int32), axis=1)
    tile_expert = jnp.minimum(tile_expert, N_EXPERTS - 1)
    filler = jnp.arange(n_tiles * tm, dtype=jnp.int32) % (n // 2)
    row_token = filler.at[pos].set(jnp.arange(n, dtype=jnp.int32) // 2)
    return pos, row_token, tile_expert, n_used.astype(jnp.int32).reshape(1)


def _layer(x, c, rel_bias, w_mod, b_mod, attn_pre_gain, attn_post_gain, w_in, shift_mu, w_decay_up, decay_bias,
           w_aicl_up, aicl_bias, w_gate_up, k_k, k_a, r_k, gn_w, gn_b, lam_q1, lam_k1, lam_q2, lam_k2, subln_w,
           w_out, ffn_pre_gain, ffn_post_gain, w_coarse, b_coarse, w_fine, b_fine, w_exp_gate, w_exp_up,
           w_exp_down, *, layer_index, tm_proj, tq, rsub_attn, tm_moe):
    B, S, D = x.shape
    T = B * S
    W = w_decay_up.shape[1]
    rw_cols = shift_mu.shape[0]
    lambda_init = 0.8 - 0.6 * math.exp(-0.3 * layer_index)
    x2 = x.reshape(T, D)

    mod = _mod(c, w_mod, b_mod)
    sh_a, sc_a, g_a, sh_f, sc_f, g_f = jnp.split(mod, 6, axis=-1)

    q_cols = DIFF_HEADS * 2 * DIFF_HEAD_DIM
    q_scale = DIFF_HEAD_DIM ** -0.5 * LOG2E
    w_rw_b = w_in[:, :rw_cols].astype(BF16)
    w_da_b = jnp.concatenate([w_in[:, rw_cols:rw_cols + q_cols] * q_scale, w_in[:, rw_cols + q_cols:]],
                             axis=1).astype(BF16)
    p_rw = _inproj(x2, attn_pre_gain, sc_a, sh_a, w_rw_b, shift_mu,
                   shift=True, out_dtype=F32, seq=S, tm=tm_proj)
    p_da = _inproj(x2, attn_pre_gain, sc_a, sh_a, w_da_b, jnp.zeros((w_in.shape[1] - rw_cols,), F32),
                   shift=False, out_dtype=BF16, seq=S, tm=tm_proj)

    o_rw = _rwkv(p_rw, w_decay_up, decay_bias, w_aicl_up, aicl_bias, w_gate_up, k_k, k_a, r_k.reshape(-1),
                 gn_w, gn_b, seq=S)
    o_da = _diff_attention(p_da, rel_bias, lam_q1, lam_k1, lam_q2, lam_k2, subln_w, lambda_init, seq=S, tq=tq,
                           rsub=rsub_attn)

    w_route = jnp.zeros((D, LANES), F32).at[:, :N_GROUPS].set(w_coarse).at[:, N_GROUPS:N_GROUPS + N_EXPERTS].set(w_fine)
    b_route = jnp.zeros((1, LANES), F32).at[0, :N_GROUPS].set(b_coarse).at[0, N_GROUPS:N_GROUPS + N_EXPERTS].set(b_fine)
    x1, h2, route = _outproj(o_rw, o_da, x2, w_out.astype(BF16), attn_post_gain, g_a, ffn_pre_gain, sc_f, sh_f,
                             w_route, b_route, seq=S, tm=min(2 * tm_proj, S))

    expert_flat = route[:, 0:2].astype(jnp.int32).reshape(-1)
    n_tiles = (2 * T) // tm_moe + N_EXPERTS
    pos, row_token, tile_expert, n_used = _dispatch_plan(expert_flat, tm_moe, n_tiles)
    gather_steps_used = (n_used * tm_moe + GATHER_ROWS - 1) // GATHER_ROWS
    xs = _gather_rows(h2, row_token, gather_steps_used, tg=GATHER_ROWS)
    ys = _experts(xs, tile_expert, n_used, w_exp_gate, w_exp_up, w_exp_down, tm=tm_moe)
    out = _combine(ys, pos, route, x1, ffn_post_gain, g_f, seq=S, tm=COMBINE_ROWS)
    return out.reshape(B, S, D)


def kernel(x, c, rel_bias, w_mod, b_mod, attn_pre_gain, attn_post_gain, w_in, shift_mu, w_decay_up, decay_bias,
           w_aicl_up, aicl_bias, w_gate_up, k_k, k_a, r_k, gn_w, gn_b, lam_q1, lam_k1, lam_q2, lam_k2, subln_w,
           w_out, ffn_pre_gain, ffn_post_gain, w_coarse, b_coarse, w_fine, b_fine, w_exp_gate, w_exp_up,
           w_exp_down):
    depth = w_mod.shape[0]
    S = x.shape[1]
    tm_proj = min(256, S)
    tq = min(1024, S)
    tm_moe = 512 if S >= 4096 else 128
    for l in range(depth):
        x = _layer(x, c, rel_bias, w_mod[l], b_mod[l], attn_pre_gain[l], attn_post_gain[l], w_in[l], shift_mu[l],
                   w_decay_up[l], decay_bias[l], w_aicl_up[l], aicl_bias[l], w_gate_up[l], k_k[l], k_a[l], r_k[l],
                   gn_w[l], gn_b[l], lam_q1[l], lam_k1[l], lam_q2[l], lam_k2[l], subln_w[l], w_out[l],
                   ffn_pre_gain[l], ffn_post_gain[l], w_coarse[l], b_coarse[l], w_fine[l], b_fine[l],
                   w_exp_gate[l], w_exp_up[l], w_exp_down[l], layer_index=l, tm_proj=tm_proj, tq=tq,
                   rsub_attn=min(256, tq), tm_moe=tm_moe)
    return x
```

```python
import functools
import math

import jax
import jax.numpy as jnp
from jax import lax
from jax.experimental import pallas as pl
from jax.experimental.pallas import tpu as pltpu

F32 = jnp.float32
BF16 = jnp.bfloat16

RWKV_HEAD_DIM = 64
DECAY_RANK = 64
AICL_RANK = 64
GATE_RANK = 128
RWKV_GN_EPS = 64e-5
DIFF_HEADS = 8
DIFF_HEAD_DIM = 64
N_BUCKETS = 32
MAX_DISTANCE = 128
SUBLN_EPS = 1e-5
N_GROUPS = 8
EXPERTS_PER_GROUP = 8
N_EXPERTS = N_GROUPS * EXPERTS_PER_GROUP
NORM_EPS = 1e-6
MASK_VALUE = -1e30
LOG2E = math.log2(math.e)

LANES = 128
RWKV_CHUNK = 64
VMEM_LIMIT = 56 * 1024 * 1024
GATHER_ROWS = 1024
COMBINE_ROWS = 512
ISSUE_UNROLL = 8
EXPERT_SPLIT = 2


def _cparams(sem):
    return pltpu.CompilerParams(dimension_semantics=sem, vmem_limit_bytes=VMEM_LIMIT)


def _dot(a, b):
    return jnp.dot(a, b, preferred_element_type=F32)


def _dot_nt(a, b):
    return lax.dot_general(a, b, (((1,), (1,)), ((), ())), preferred_element_type=F32)


def _dot_tn(a, b):
    return lax.dot_general(a, b, (((0,), (0,)), ((), ())), preferred_element_type=F32)


def _mod_kernel(c_ref, w_ref, b_ref, o_ref):
    c = c_ref[...]
    s = c * jax.nn.sigmoid(c)
    o_ref[...] = _dot(s, w_ref[...]) + b_ref[...]


def _mod(c, w_mod, b_mod, tn=1024):
    B, D = c.shape
    N = w_mod.shape[1]
    rows = 8
    c8 = jnp.zeros((rows, D), F32).at[:B].set(c)
    out = pl.pallas_call(
        _mod_kernel,
        grid=(N // tn,),
        in_specs=[pl.BlockSpec((rows, D), lambda j: (0, 0)),
                  pl.BlockSpec((D, tn), lambda j: (0, j)),
                  pl.BlockSpec((1, tn), lambda j: (0, j))],
        out_specs=pl.BlockSpec((rows, tn), lambda j: (0, j)),
        out_shape=jax.ShapeDtypeStruct((rows, N), F32),
        compiler_params=_cparams(("arbitrary",)),
        name="mod",
    )(c8, w_mod, b_mod.reshape(1, N))
    return out[:B]


def _inproj_kernel(x_ref, gain_ref, sc_ref, sh_ref, w_ref, mu_ref, o_ref, carry_ref, *, shift, tiles_per_batch):
    i = pl.program_id(0)
    x = x_ref[...]
    ms = jnp.mean(x * x, axis=-1, keepdims=True)
    h = x * lax.rsqrt(ms + NORM_EPS) * gain_ref[...]
    h = h * (1.0 + sc_ref[0]) + sh_ref[0]
    p = _dot(h.astype(BF16), w_ref[...])
    if shift:
        tm = p.shape[0]

        @pl.when(i % tiles_per_batch == 0)
        def _():
            carry_ref[...] = jnp.zeros_like(carry_ref)

        prev = pltpu.roll(p, 1, axis=0)
        row = lax.broadcasted_iota(jnp.int32, p.shape, 0)
        prev = jnp.where(row == 0, carry_ref[0:1, :], prev)
        carry_ref[0:1, :] = p[tm - 1:tm, :]
        p = p + (prev - p) * mu_ref[...]
    o_ref[...] = p.astype(o_ref.dtype)


def _inproj(x2, gain, sc, sh, w_bf16, mu, *, shift, out_dtype, seq, tm):
    T, D = x2.shape
    N = w_bf16.shape[1]
    B = T // seq
    tiles_per_batch = seq // tm
    kern = functools.partial(_inproj_kernel, shift=shift, tiles_per_batch=tiles_per_batch)
    return pl.pallas_call(
        kern,
        grid=(T // tm,),
        in_specs=[pl.BlockSpec((tm, D), lambda i: (i, 0)),
                  pl.BlockSpec((1, D), lambda i: (0, 0)),
                  pl.BlockSpec((1, 1, D), lambda i: (i // tiles_per_batch, 0, 0)),
                  pl.BlockSpec((1, 1, D), lambda i: (i // tiles_per_batch, 0, 0)),
                  pl.BlockSpec((D, N), lambda i: (0, 0)),
                  pl.BlockSpec((1, N), lambda i: (0, 0))],
        out_specs=pl.BlockSpec((tm, N), lambda i: (i, 0)),
        out_shape=jax.ShapeDtypeStruct((T, N), out_dtype),
        scratch_shapes=[pltpu.VMEM((8, N), F32)],
        compiler_params=_cparams(("arbitrary",)),
        name="inproj_shift" if shift else "inproj",
    )(x2, gain.reshape(1, D), sc.reshape(B, 1, D), sh.reshape(B, 1, D), w_bf16, mu.reshape(1, N))


def _rwkv_kernel(p_ref, wdu_ref, db_ref, wau_ref, ab_ref, wgu_ref, kk_ref, ka_ref, rk_ref,
                 gnw_ref, gnb_ref, o_ref, s_ref, *, width):
    L = RWKV_CHUNK
    W = width
    n_pairs = W // LANES
    nb = p_ref.shape[0]
    c = pl.program_id(0)

    @pl.when(c == 0)
    def _():
        s_ref[...] = jnp.zeros_like(s_ref)

    cols = lambda c0, c1: p_ref[:, :, c0:c1].reshape(nb * L, c1 - c0)
    r = cols(0, W)
    k = cols(W, 2 * W)
    v = cols(2 * W, 3 * W)
    xw = cols(3 * W, 3 * W + DECAY_RANK)
    xa = cols(3 * W + DECAY_RANK, 3 * W + DECAY_RANK + AICL_RANK)
    xg = cols(3 * W + DECAY_RANK + AICL_RANK, 3 * W + DECAY_RANK + AICL_RANK + GATE_RANK)

    d = db_ref[...] + _dot(jnp.tanh(xw).astype(BF16), wdu_ref[...].astype(BF16))
    lw = -math.exp(-0.5) * jax.nn.sigmoid(d)
    a = jax.nn.sigmoid(ab_ref[...] + _dot(xa.astype(BF16), wau_ref[...].astype(BF16)))
    g = _dot(jax.nn.sigmoid(xg).astype(BF16), wgu_ref[...].astype(BF16))
    kkf = k * kk_ref[...]
    kmod = k * (1.0 + (a - 1.0) * ka_ref[...])

    ri = lax.broadcasted_iota(jnp.int32, (nb * L, nb * L), 0)
    ci = lax.broadcasted_iota(jnp.int32, (nb * L, nb * L), 1)
    tri = ((ri >= ci) & (ri // L == ci // L)).astype(BF16)
    lw_hi = lw.astype(BF16)
    rem = lw - lw_hi.astype(F32)
    lw_mid = rem.astype(BF16)
    lw_lo = (rem - lw_mid.astype(F32)).astype(BF16)
    cin = _dot(tri, lw_hi) + _dot(tri, lw_mid) + _dot(tri, lw_lo)
    cex = cin - lw
    c_last = [cin[(b + 1) * L - 1:(b + 1) * L, :] for b in range(nb)]
    e_in = jnp.exp(cin)
    e_ni = jnp.exp(-cin)
    e_ex = jnp.exp(cex)
    e_l = jnp.exp(jnp.concatenate([jnp.broadcast_to(cl, (L, W)) for cl in c_last], axis=0) - cin)
    g_last = [jnp.exp(cl) for cl in c_last]

    lane = lax.broadcasted_iota(jnp.int32, (L, LANES), 1)
    lo = lane < RWKV_HEAD_DIM
    i2 = lax.broadcasted_iota(jnp.int32, (2 * L, LANES), 0)
    j2 = lax.broadcasted_iota(jnp.int32, (2 * L, LANES), 1)
    same_blk = (i2 // L) == (j2 // RWKV_HEAD_DIM)
    mask_strict = same_blk & (i2 > j2)
    mask_incl = same_blk & (i2 >= j2)
    eye = (i2 == j2).astype(F32)

    def stack(t):
        return jnp.concatenate([jnp.where(lo, t, 0.0), jnp.where(lo, 0.0, t)], axis=0)

    def dup(t):
        return jnp.concatenate([t, t], axis=0)

    def headsum(t):
        s_lo = jnp.sum(jnp.where(lo, t, 0.0), axis=-1, keepdims=True)
        s_hi = jnp.sum(jnp.where(lo, 0.0, t), axis=-1, keepdims=True)
        return jnp.where(lo, s_lo, s_hi)

    inv_n = 1.0 / RWKV_HEAD_DIM
    chains = [(b * n_pairs + hp, b, slice(b * L, (b + 1) * L), slice(hp * LANES, (hp + 1) * LANES))
              for b in range(nb) for hp in range(n_pairs)]
    pairs = range(len(chains))
    lh, rh, v_st, bk = [], [], [], []
    for _, _, rb, sl in chains:
        kk_p = kkf[rb, sl]
        nrm = jnp.maximum(jnp.sqrt(headsum(kk_p * kk_p)), 1e-12)
        kn = kk_p / nrm
        bp = kn * a[rb, sl]
        ag = stack(-kn * e_ex[rb, sl])
        rg = stack(r[rb, sl] * e_in[rb, sl])
        bd = dup(bp * e_ni[rb, sl])
        kd = dup(kmod[rb, sl] * e_ni[rb, sl])
        bl = stack(bp * e_l[rb, sl])
        kl = stack(kmod[rb, sl] * e_l[rb, sl])
        v_st.append(stack(v[rb, sl]).astype(BF16))
        lh.append(jnp.concatenate([ag, rg], axis=0).astype(BF16))
        rh.append(jnp.concatenate([bd, kd], axis=0).astype(BF16))
        bk.append(jnp.concatenate([bl, kl], axis=0).astype(BF16))
    a4 = [_dot_nt(lh[i], rh[i]) for i in pairs]
    a_ab = [jnp.where(mask_strict, a4[i][0:2 * L, 0:2 * L], 0.0) for i in pairs]
    a_ak = [jnp.where(mask_strict, a4[i][0:2 * L, 2 * L:4 * L], 0.0).astype(BF16) for i in pairs]
    a_rb = [jnp.where(mask_incl, a4[i][2 * L:4 * L, 0:2 * L], 0.0).astype(BF16) for i in pairs]
    a_rk = [jnp.where(mask_incl, a4[i][2 * L:4 * L, 2 * L:4 * L], 0.0).astype(BF16) for i in pairs]

    pw = a_ab
    tinv = [eye + a_ab[i] for i in pairs]
    for _ in range(int(math.log2(L)) - 1):
        pb = [pw[i].astype(BF16) for i in pairs]
        pw = [_dot(pb[i], pb[i]) for i in pairs]
        tinv = [tinv[i] + _dot(tinv[i].astype(BF16), pw[i].astype(BF16)) for i in pairs]

    s0 = [s_ref[chains[i][0]] for i in pairs]
    xs = [_dot_nt(lh[i], s0[i].astype(BF16)) for i in pairs]
    akv = [_dot(a_ak[i], v_st[i]) for i in pairs]
    u_b = [_dot(tinv[i].astype(BF16), (xs[i][0:2 * L] + akv[i]).astype(BF16)).astype(BF16) for i in pairs]
    y = [xs[i][2 * L:4 * L] + _dot(a_rb[i], u_b[i]) + _dot(a_rk[i], v_st[i]) for i in pairs]
    for i, (si, b, rb, sl) in enumerate(chains):
        uv = jnp.concatenate([u_b[i], v_st[i]], axis=0)
        s_ref[si] = s0[i] * g_last[b][:, sl] + _dot_tn(uv, bk[i])

    for i, (si, b, rb, sl) in enumerate(chains):
        mu = jnp.sum(y[i], axis=-1, keepdims=True) * inv_n
        dlt = jnp.where(same_blk, y[i] - mu, 0.0)
        var = jnp.sum(dlt * dlt, axis=-1, keepdims=True) * inv_n
        yn = dlt * lax.rsqrt(var + RWKV_GN_EPS)
        y_p = (yn[0:L] + yn[L:2 * L]) * gnw_ref[:, sl] + gnb_ref[:, sl]
        bonus = headsum(r[rb, sl] * kmod[rb, sl] * rk_ref[:, sl]) * v[rb, sl]
        o_ref[b, :, sl] = ((y_p + bonus) * g[rb, sl]).astype(o_ref.dtype)


def _rwkv(p_rw, w_decay_up, decay_bias, w_aicl_up, aicl_bias, w_gate_up, k_k, k_a, r_k, gn_w, gn_b, *, seq):
    T, C = p_rw.shape
    W = w_decay_up.shape[1]
    B = T // seq
    L = RWKV_CHUNK
    nC = seq // L
    row = lambda t: t.reshape(1, W)
    full = lambda shape: pl.BlockSpec(shape, lambda c: (0,) * len(shape))
    out = pl.pallas_call(
        functools.partial(_rwkv_kernel, width=W),
        grid=(nC,),
        in_specs=[pl.BlockSpec((B, L, C), lambda c: (0, c, 0)),
                  full((DECAY_RANK, W)), full((1, W)), full((AICL_RANK, W)), full((1, W)),
                  full((GATE_RANK, W)), full((1, W)), full((1, W)), full((1, W)), full((1, W)), full((1, W))],
        out_specs=pl.BlockSpec((B, L, W), lambda c: (0, c, 0)),
        out_shape=jax.ShapeDtypeStruct((B, seq, W), BF16),
        scratch_shapes=[pltpu.VMEM((B * (W // LANES), LANES, LANES), F32)],
        compiler_params=_cparams(("arbitrary",)),
        name="rwkv",
    )(p_rw.reshape(B, seq, C), w_decay_up, row(decay_bias), w_aicl_up, row(aicl_bias), w_gate_up, row(k_k),
      row(k_a), row(r_k), row(gn_w), row(gn_b))
    return out.reshape(T, W)


def _t5_bucket(rel):
    n = jnp.maximum(rel, 0)
    max_exact = N_BUCKETS // 2
    nf = jnp.maximum(n, 1).astype(F32)
    large = max_exact + (jnp.log(nf / max_exact) / math.log(MAX_DISTANCE / max_exact)
                         * (N_BUCKETS - max_exact)).astype(jnp.int32)
    large = jnp.minimum(large, N_BUCKETS - 1)
    return jnp.where(n < max_exact, n, large)


def _attn_kernel(q_ref, k_ref, v_ref, tb_ref, lq1_ref, lk1_ref, lq2_ref, lk2_ref,
                 sw_ref, o_ref, qs_ref, m_ref, l_ref, acc_ref, *, tq, rsub, lambda_init):
    qi = pl.program_id(2)
    tk = tq

    q = q_ref[...]
    lane = lax.broadcasted_iota(jnp.int32, q.shape, 1)
    zero = jnp.zeros_like(q)
    qs_ref[0:tq, :] = jnp.where(lane < DIFF_HEAD_DIM, q, zero)
    qs_ref[tq:2 * tq, :] = jnp.where(lane < DIFF_HEAD_DIM, zero, q)
    m_ref[...] = jnp.full_like(m_ref, MASK_VALUE)
    l_ref[...] = jnp.zeros_like(l_ref)
    acc_ref[...] = jnp.zeros_like(acc_ref)

    band = tb_ref.shape[2]

    def add_bias(s, bias):
        nb = bias.shape[1]
        if nb == s.shape[1]:
            return s + bias
        return jnp.concatenate([s[:, :s.shape[1] - nb], s[:, s.shape[1] - nb:] + bias], axis=1)

    def tile(kj, mode):
        koff = pl.multiple_of(kj * tk, tk)
        subs = [slice(sb * rsub, (sb + 1) * rsub) for sb in range(2 * tq // rsub)]
        r0s = [(sb * rsub) % tq for sb in range(len(subs))]
        widths = [r0 + rsub if mode == "diag" else tk for r0 in r0s]
        m_prev = [m_ref[rows, :] for rows in subs]
        l_prev = [l_ref[rows, :] for rows in subs]
        acc_prev = [acc_ref[rows, :] for rows in subs]
        s_all = [_dot_nt(qs_ref[rows, :], k_ref[pl.ds(koff, w), :]) for rows, w in zip(subs, widths)]
        m_out, l_out, alphas, ps = [], [], [], []
        for sb, s in enumerate(s_all):
            if mode == "diag":
                s = add_bias(s, tb_ref[0] if r0s[sb] > 0 else tb_ref[0, :, band - rsub:])
            elif mode == "near" and r0s[sb] == 0:
                s = add_bias(s, tb_ref[0, :, 0:LANES])
            m_new = jnp.maximum(m_prev[sb], jnp.max(s, axis=-1, keepdims=True))
            alpha = jnp.exp2(m_prev[sb] - m_new)
            p = jnp.exp2(s - jnp.concatenate([m_new] * (widths[sb] // LANES), axis=1))
            m_out.append(m_new)
            l_out.append(alpha * l_prev[sb] + jnp.sum(p, axis=-1, keepdims=True))
            alphas.append(alpha)
            ps.append(p.astype(BF16))
        pv = [_dot(p, v_ref[pl.ds(koff, w), :]) for p, w in zip(ps, widths)]
        for sb, rows in enumerate(subs):
            m_ref[rows, :] = m_out[sb]
            l_ref[rows, :] = l_out[sb]
            acc_ref[rows, :] = alphas[sb] * acc_prev[sb] + pv[sb]

    def far_tile(kj, carry):
        tile(kj, "far")
        return carry

    lax.fori_loop(0, jnp.maximum(qi - 1, 0), far_tile, 0)

    @pl.when(qi >= 1)
    def _():
        tile(qi - 1, "near")

    tile(qi, "diag")
    l = l_ref[...]
    acc = acc_ref[...]
    o1 = acc[0:tq] / l[0:tq]
    o2 = acc[tq:2 * tq] / l[tq:2 * tq]
    lam = (jnp.exp(jnp.sum(lq1_ref[...] * lk1_ref[...], axis=-1, keepdims=True))
           - jnp.exp(jnp.sum(lq2_ref[...] * lk2_ref[...], axis=-1, keepdims=True)) + lambda_init)
    o = o1 - lam * o2
    o = o * lax.rsqrt(jnp.mean(o * o, axis=-1, keepdims=True) + SUBLN_EPS)
    o = o * sw_ref[...] * (1.0 - lambda_init)
    o_ref[...] = o.astype(o_ref.dtype)


def _diff_attention(p_da, rel_bias, lam_q1, lam_k1, lam_q2, lam_k2, subln_w, lambda_init, *, seq, tq, rsub):
    T, C = p_da.shape
    H = DIFF_HEADS
    B = T // seq
    nq = seq // tq
    hd2 = 2 * DIFF_HEAD_DIM
    assert rsub % LANES == 0 and tq % rsub == 0 and MAX_DISTANCE <= LANES
    far = rel_bias[N_BUCKETS - 1]
    band = rsub + LANES
    L = rsub + band
    f = ((rel_bias[_t5_bucket(jnp.arange(L))] - far) * LOG2E).T.astype(F32)
    w = jnp.concatenate([f[:, 0:LANES + 1][:, ::-1], jnp.full((H, band - LANES - 1), MASK_VALUE, F32),
                         f[:, LANES + 1:LANES + rsub + 1][:, ::-1]], axis=1)
    bias_tb = jnp.tile(w, (1, rsub))[:, :rsub * (L - 1)].reshape(H, rsub, L - 1)[:, :, :band]
    vec = lambda t: t.reshape(1, -1)
    small = lambda n: pl.BlockSpec((1, n), lambda b, h, i: (0, 0))
    return pl.pallas_call(
        functools.partial(_attn_kernel, tq=tq, rsub=rsub, lambda_init=lambda_init),
        grid=(B, H, nq),
        in_specs=[pl.BlockSpec((tq, hd2), lambda b, h, i: (b * nq + i, h)),
                  pl.BlockSpec((seq, hd2), lambda b, h, i: (b, H + h)),
                  pl.BlockSpec((seq, hd2), lambda b, h, i: (b, 2 * H + h)),
                  pl.BlockSpec((1, rsub, band), lambda b, h, i: (h, 0, 0)),
                  small(DIFF_HEAD_DIM), small(DIFF_HEAD_DIM), small(DIFF_HEAD_DIM), small(DIFF_HEAD_DIM),
                  small(hd2)],
        out_specs=pl.BlockSpec((tq, hd2), lambda b, h, i: (b * nq + i, h)),
        out_shape=jax.ShapeDtypeStruct((T, H * hd2), BF16),
        scratch_shapes=[pltpu.VMEM((2 * tq, hd2), BF16), pltpu.VMEM((2 * tq, LANES), F32),
                        pltpu.VMEM((2 * tq, LANES), F32), pltpu.VMEM((2 * tq, hd2), F32)],
        compiler_params=_cparams(("arbitrary", "arbitrary", "arbitrary")),
        name="diff_attn",
    )(p_da, p_da, p_da, bias_tb, vec(lam_q1), vec(lam_k1), vec(lam_q2), vec(lam_k2), vec(subln_w))


def _outproj_kernel(orw_ref, oda_ref, x_ref, w1_ref, w2_ref, pg_ref, ga_ref, fg_ref, scf_ref, shf_ref,
                    wr_ref, br_ref, x1_ref, h2_ref, route_ref):
    mix = _dot(orw_ref[...], w1_ref[...]) + _dot(oda_ref[...], w2_ref[...])
    y = mix * lax.rsqrt(jnp.mean(mix * mix, axis=-1, keepdims=True) + NORM_EPS) * pg_ref[...]
    x1 = x_ref[...] + ga_ref[0] * y
    x1_ref[...] = x1
    h2 = x1 * lax.rsqrt(jnp.mean(x1 * x1, axis=-1, keepdims=True) + NORM_EPS) * fg_ref[...]
    h2 = h2 * (1.0 + scf_ref[0]) + shf_ref[0]
    half = h2.shape[1] // 2
    hi = lax.bitcast_convert_type(h2[:, :half].astype(BF16).astype(F32), jnp.uint32)
    lo = lax.bitcast_convert_type(h2[:, half:].astype(BF16).astype(F32), jnp.uint32)
    h2_ref[...] = hi | (lo >> 16)

    h_hi = h2.astype(BF16)
    h_lo = (h2 - h_hi.astype(F32)).astype(BF16)
    logits = (_dot(h_hi, wr_ref[0]) + (_dot(h_hi, wr_ref[1]) + _dot(h_lo, wr_ref[0]) + _dot(h_lo, wr_ref[1]))
              + br_ref[...])
    lane = lax.broadcasted_iota(jnp.int32, logits.shape, 1).astype(F32)
    big = float(LANES)
    neg = -jnp.inf
    cl = jnp.where(lane < N_GROUPS, logits, neg)
    cmax = jnp.max(cl, axis=-1, keepdims=True)
    grp = jnp.min(jnp.where(cl == cmax, lane, big), axis=-1, keepdims=True)
    grp_p = 1.0 / jnp.sum(jnp.exp(cl - cmax), axis=-1, keepdims=True)
    f_lo = N_GROUPS + EXPERTS_PER_GROUP * grp
    fl = jnp.where((lane >= f_lo) & (lane < f_lo + EXPERTS_PER_GROUP), logits, neg)
    v1 = jnp.max(fl, axis=-1, keepdims=True)
    i1 = jnp.min(jnp.where(fl == v1, lane, big), axis=-1, keepdims=True)
    fl2 = jnp.where(lane == i1, neg, fl)
    v2 = jnp.max(fl2, axis=-1, keepdims=True)
    i2 = jnp.min(jnp.where(fl2 == v2, lane, big), axis=-1, keepdims=True)
    e21 = jnp.exp(v2 - v1)
    w1 = grp_p / (1.0 + e21)
    w2 = w1 * e21
    route = jnp.where(lane == 0, i1 - N_GROUPS,
                      jnp.where(lane == 1, i2 - N_GROUPS,
                                jnp.where(lane == 2, w1, jnp.where(lane == 3, w2, 0.0))))
    route_ref[...] = route


def _outproj(o_rw, o_da, x2, w_out_bf16, post_gain, g_a, ffn_gain, sc_f, sh_f, w_route, b_route, *, seq, tm):
    T, D = x2.shape
    W1 = o_rw.shape[1]
    W2 = o_da.shape[1]
    B = T // seq
    tpb = seq // tm
    rowD = lambda: pl.BlockSpec((1, D), lambda i: (0, 0))
    perb = lambda: pl.BlockSpec((1, 1, D), lambda i: (i // tpb, 0, 0))
    r3 = lambda t: t.reshape(B, 1, D)
    return pl.pallas_call(
        _outproj_kernel,
        grid=(T // tm,),
        in_specs=[pl.BlockSpec((tm, W1), lambda i: (i, 0)),
                  pl.BlockSpec((tm, W2), lambda i: (i, 0)),
                  pl.BlockSpec((tm, D), lambda i: (i, 0)),
                  pl.BlockSpec((W1, D), lambda i: (0, 0)),
                  pl.BlockSpec((W2, D), lambda i: (1, 0)),
                  rowD(), perb(), rowD(), perb(), perb(),
                  pl.BlockSpec((2, D, LANES), lambda i: (0, 0, 0)),
                  pl.BlockSpec((1, LANES), lambda i: (0, 0))],
        out_specs=[pl.BlockSpec((tm, D), lambda i: (i, 0)),
                   pl.BlockSpec((tm, D // 2), lambda i: (i, 0)),
                   pl.BlockSpec((tm, LANES), lambda i: (i, 0))],
        out_shape=[jax.ShapeDtypeStruct((T, D), F32),
                   jax.ShapeDtypeStruct((T, D // 2), jnp.uint32),
                   jax.ShapeDtypeStruct((T, LANES), F32)],
        compiler_params=_cparams(("arbitrary",)),
        name="outproj_router",
    )(o_rw, o_da, x2, w_out_bf16, w_out_bf16, post_gain.reshape(1, D), r3(g_a), ffn_gain.reshape(1, D),
      r3(sc_f), r3(sh_f), w_route, b_route)


def _new_expert(te_ref, i):
    return (i == 0) | (te_ref[i] != te_ref[jnp.maximum(i - 1, 0)])


def _expert_up_kernel(te_ref, nu_ref, x_ref, wg_ref, wu_ref, o_ref, wgb_ref, wub_ref):
    i = pl.program_id(1)

    @pl.when(i < nu_ref[0])
    def _():
        @pl.when(_new_expert(te_ref, i))
        def _():
            wgb_ref[...] = wg_ref[0].astype(BF16)
            wub_ref[...] = wu_ref[0].astype(BF16)

        xw = x_ref[...]
        half = xw.shape[1]
        xa = lax.bitcast_convert_type(xw & jnp.uint32(0xFFFF0000), F32).astype(BF16)
        xb = lax.bitcast_convert_type(xw << 16, F32).astype(BF16)
        gt = _dot(xa, wgb_ref[0:half, :]) + _dot(xb, wgb_ref[half:2 * half, :])
        up = _dot(xa, wub_ref[0:half, :]) + _dot(xb, wub_ref[half:2 * half, :])
        o_ref[...] = (gt * jax.nn.sigmoid(gt) * up).astype(o_ref.dtype)

    @pl.when(i >= nu_ref[0])
    def _():
        o_ref[...] = jnp.zeros_like(o_ref)


def _expert_down_kernel(te_ref, nu_ref, a_ref, wd_ref, o_ref, wdb_ref):
    i = pl.program_id(1)

    @pl.when(i < nu_ref[0])
    def _():
        @pl.when(_new_expert(te_ref, i))
        def _():
            wdb_ref[...] = wd_ref[0].astype(BF16)

        y = _dot(a_ref[...], wdb_ref[...])
        q = y.shape[1] // 2
        hi = lax.bitcast_convert_type(y[:, :q].astype(BF16).astype(F32), jnp.uint32)
        lo = lax.bitcast_convert_type(y[:, q:].astype(BF16).astype(F32), jnp.uint32)
        o_ref[...] = hi | (lo >> 16)

    @pl.when(i >= nu_ref[0])
    def _():
        o_ref[...] = jnp.zeros_like(o_ref)


def _experts(xs, tile_expert, n_used, wg, wu, wd, *, tm, n_split=EXPERT_SPLIT):
    P = xs.shape[0]
    D, F = wg.shape[1], wg.shape[2]
    n_tiles = P // tm
    fh = F // n_split
    dh = D // n_split
    row = lambda j, i, te, nu: jnp.minimum(i, nu[0] - 1)
    act = pl.pallas_call(
        _expert_up_kernel,
        grid_spec=pltpu.PrefetchScalarGridSpec(
            num_scalar_prefetch=2,
            grid=(n_split, n_tiles),
            in_specs=[pl.BlockSpec((tm, D // 2), lambda j, i, te, nu: (row(j, i, te, nu), 0)),
                      pl.BlockSpec((1, D, fh), lambda j, i, te, nu: (te[i], 0, j)),
                      pl.BlockSpec((1, D, fh), lambda j, i, te, nu: (te[i], 0, j))],
            out_specs=pl.BlockSpec((tm, fh), lambda j, i, te, nu: (i, j)),
            scratch_shapes=[pltpu.VMEM((D, fh), BF16), pltpu.VMEM((D, fh), BF16)],
        ),
        out_shape=jax.ShapeDtypeStruct((P, F), BF16),
        compiler_params=_cparams(("arbitrary", "arbitrary")),
        name="experts_up",
    )(tile_expert, n_used, xs, wg, wu)
    return pl.pallas_call(
        _expert_down_kernel,
        grid_spec=pltpu.PrefetchScalarGridSpec(
            num_scalar_prefetch=2,
            grid=(n_split, n_tiles),
            in_specs=[pl.BlockSpec((tm, F), lambda j, i, te, nu: (row(j, i, te, nu), 0)),
                      pl.BlockSpec((1, F, dh), lambda j, i, te, nu: (te[i], 0, j))],
            out_specs=pl.BlockSpec((tm, dh // 2), lambda j, i, te, nu: (i, j)),
            scratch_shapes=[pltpu.VMEM((F, dh), BF16)],
        ),
        out_shape=jax.ShapeDtypeStruct((P, D // 2), jnp.uint32),
        compiler_params=_cparams(("arbitrary", "arbitrary")),
        name="experts_down",
    )(tile_expert, n_used, act, wd)


def _row_copy(src_ref, row, dst_ref, slot, sem):
    return pltpu.make_async_copy(src_ref.at[row], dst_ref.at[slot], sem)


def _wait_rows(src_ref, dst_ref, sem):
    pltpu.make_async_copy(src_ref.at[pl.ds(0, dst_ref.shape[0])], dst_ref, sem).wait()


def _gather_kernel(ns_ref, idx_ref, src_ref, o_ref, sem):
    i = pl.program_id(0)

    @pl.when(i < ns_ref[0])
    def _():
        rows = o_ref.shape[0]

        def issue(g, carry):
            for u in range(ISSUE_UNROLL):
                r = g * ISSUE_UNROLL + u
                _row_copy(src_ref, idx_ref[r], o_ref, r, sem.at[u % 2]).start(priority=u % 2)
            return carry

        lax.fori_loop(0, rows // ISSUE_UNROLL, issue, 0)
        for prio in range(2):
            _wait_rows(src_ref, o_ref.at[pl.ds(0, rows // 2)], sem.at[prio])

    @pl.when(i >= ns_ref[0])
    def _():
        o_ref[...] = jnp.zeros_like(o_ref)


def _gather_rows(src, idx, n_steps_used, *, tg):
    P = idx.shape[0]
    D = src.shape[1]
    out_dtype = src.dtype
    return pl.pallas_call(
        _gather_kernel,
        grid_spec=pltpu.PrefetchScalarGridSpec(
            num_scalar_prefetch=1,
            grid=(P // tg,),
            in_specs=[pl.BlockSpec((tg,), lambda i, ns: (i,), memory_space=pltpu.SMEM),
                      pl.BlockSpec(memory_space=pl.ANY)],
            out_specs=pl.BlockSpec((tg, D), lambda i, ns: (i, 0)),
            scratch_shapes=[pltpu.SemaphoreType.DMA((2,))],
        ),
        out_shape=jax.ShapeDtypeStruct((P, D), out_dtype),
        compiler_params=pltpu.CompilerParams(dimension_semantics=("arbitrary",), vmem_limit_bytes=VMEM_LIMIT,
                                             disable_bounds_checks=True),
        name="dispatch_gather",
    )(n_steps_used, idx, src)


def _combine_kernel(pos_ref, ys_ref, route_ref, x1_ref, pg_ref, gf_ref, o_ref, y0_ref, y1_ref, sem0, sem1, *,
                    n_split):
    tm = x1_ref.shape[0]

    def issue(g, carry):
        for u in range(ISSUE_UNROLL):
            t = g * ISSUE_UNROLL + u
            _row_copy(ys_ref, pos_ref[2 * t], y0_ref, t, sem0).start(priority=0)
            _row_copy(ys_ref, pos_ref[2 * t + 1], y1_ref, t, sem1).start(priority=1)
        return carry

    lax.fori_loop(0, tm // ISSUE_UNROLL, issue, 0)
    route = route_ref[...]
    w0 = route[:, 2:3]
    w1 = route[:, 3:4]
    _wait_rows(ys_ref, y0_ref, sem0)
    _wait_rows(ys_ref, y1_ref, sem1)
    hi_of = lambda w: lax.bitcast_convert_type(w & jnp.uint32(0xFFFF0000), F32)
    lo_of = lambda w: lax.bitcast_convert_type(w << 16, F32)
    p0 = y0_ref[...]
    p1 = y1_ref[...]
    y_hi = hi_of(p0) * w0 + hi_of(p1) * w1
    y_lo = lo_of(p0) * w0 + lo_of(p1) * w1
    d_model = x1_ref.shape[1]
    ms = (jnp.sum(y_hi * y_hi, axis=-1, keepdims=True) + jnp.sum(y_lo * y_lo, axis=-1, keepdims=True)) / d_model
    scale = lax.rsqrt(ms + NORM_EPS)
    q = d_model // (2 * n_split)
    for j in range(n_split):
        for part, y_part in enumerate((y_hi, y_lo)):
            cols = slice((2 * j + part) * q, (2 * j + part + 1) * q)
            yn = y_part[:, j * q:(j + 1) * q] * scale * pg_ref[:, cols]
            o_ref[:, cols] = x1_ref[:, cols] + gf_ref[0, :, cols] * yn


def _combine(ys, pos, route, x1, post_gain, g_f, *, seq, tm):
    T, D = x1.shape
    B = T // seq
    tpb = seq // tm
    return pl.pallas_call(
        functools.partial(_combine_kernel, n_split=EXPERT_SPLIT),
        grid=(T // tm,),
        in_specs=[pl.BlockSpec((2 * tm,), lambda i: (i,), memory_space=pltpu.SMEM),
                  pl.BlockSpec(memory_space=pl.ANY),
                  pl.BlockSpec((tm, LANES), lambda i: (i, 0)),
                  pl.BlockSpec((tm, D), lambda i: (i, 0)),
                  pl.BlockSpec((1, D), lambda i: (0, 0)),
                  pl.BlockSpec((1, 1, D), lambda i: (i // tpb, 0, 0))],
        out_specs=pl.BlockSpec((tm, D), lambda i: (i, 0)),
        out_shape=jax.ShapeDtypeStruct((T, D), F32),
        scratch_shapes=[pltpu.VMEM((tm, D // 2), ys.dtype), pltpu.VMEM((tm, D // 2), ys.dtype),
                        pltpu.SemaphoreType.DMA(()), pltpu.SemaphoreType.DMA(())],
        compiler_params=pltpu.CompilerParams(dimension_semantics=("arbitrary",), vmem_limit_bytes=VMEM_LIMIT,
                                             disable_bounds_checks=True),
        name="combine",
    )(pos, ys, route, x1, post_gain.reshape(1, D), g_f.reshape(B, 1, D))


def _dispatch_plan(expert_flat, tm, n_tiles):
    n = expert_flat.shape[0]
    onehot = (expert_flat[:, None] == jnp.arange(N_EXPERTS, dtype=jnp.int32)[None, :]).astype(jnp.int32)
    csum = jnp.cumsum(onehot, axis=0)
    counts = csum[-1]
    rank = jnp.sum((csum - onehot) * onehot, axis=1)
    tiles_e = (counts + tm - 1) // tm
    tile_end = jnp.cumsum(tiles_e)
    tile_start = tile_end - tiles_e
    pos = tile_start[expert_flat] * tm + rank
    n_used = tile_end[-1]
    tile_ids = jnp.minimum(jnp.arange(n_tiles, dtype=jnp.int32), n_used - 1)
    tile_expert = jnp.sum((tile_end[None, :] <= tile_ids[:, None]).astype(jnp.int32), axis=1)
    tile_expert = jnp.minimum(tile_expert, N_EXPERTS - 1)
    filler = jnp.arange(n_tiles * tm, dtype=jnp.int32) % (n // 2)
    row_token = filler.at[pos].set(jnp.arange(n, dtype=jnp.int32) // 2)
    return pos, row_token, tile_expert, n_used.astype(jnp.int32).reshape(1)


def _layer(x, c, rel_bias, w_mod, b_mod, attn_pre_gain, attn_post_gain, w_in, shift_mu, w_decay_up, decay_bias,
           w_aicl_up, aicl_bias, w_gate_up, k_k, k_a, r_k, gn_w, gn_b, lam_q1, lam_k1, lam_q2, lam_k2, subln_w,
           w_out, ffn_pre_gain, ffn_post_gain, w_coarse, b_coarse, w_fine, b_fine, w_exp_gate, w_exp_up,
           w_exp_down, *, layer_index, tm_proj, tq, rsub_attn, tm_moe):
    B, S, D = x.shape
    T = B * S
    W = w_decay_up.shape[1]
    rw_cols = shift_mu.shape[0]
    lambda_init = 0.8 - 0.6 * math.exp(-0.3 * layer_index)
    x2 = x.reshape(T, D)

    mod = _mod(c, w_mod, b_mod)
    sh_a, sc_a, g_a, sh_f, sc_f, g_f = jnp.split(mod, 6, axis=-1)

    q_cols = DIFF_HEADS * 2 * DIFF_HEAD_DIM
    q_scale = DIFF_HEAD_DIM ** -0.5 * LOG2E
    w_rw_b = w_in[:, :rw_cols].astype(BF16)
    w_da_b = jnp.concatenate([w_in[:, rw_cols:rw_cols + q_cols] * q_scale, w_in[:, rw_cols + q_cols:]],
                             axis=1).astype(BF16)
    p_rw = _inproj(x2, attn_pre_gain, sc_a, sh_a, w_rw_b, shift_mu,
                   shift=True, out_dtype=F32, seq=S, tm=tm_proj)
    p_da = _inproj(x2, attn_pre_gain, sc_a, sh_a, w_da_b, jnp.zeros((w_in.shape[1] - rw_cols,), F32),
                   shift=False, out_dtype=BF16, seq=S, tm=tm_proj)

    o_rw = _rwkv(p_rw, w_decay_up, decay_bias, w_aicl_up, aicl_bias, w_gate_up, k_k, k_a, r_k.reshape(-1),
                 gn_w, gn_b, seq=S)
    o_da = _diff_attention(p_da, rel_bias, lam_q1, lam_k1, lam_q2, lam_k2, subln_w, lambda_init, seq=S, tq=tq,
                           rsub=rsub_attn)

    w_route = jnp.zeros((D, LANES), F32).at[:, :N_GROUPS].set(w_coarse).at[:, N_GROUPS:N_GROUPS + N_EXPERTS].set(w_fine)
    b_route = jnp.zeros((1, LANES), F32).at[0, :N_GROUPS].set(b_coarse).at[0, N_GROUPS:N_GROUPS + N_EXPERTS].set(b_fine)
    w_route_hi = w_route.astype(BF16)
    w_route = jnp.stack([w_route_hi, (w_route - w_route_hi.astype(F32)).astype(BF16)])
    x1, h2, route = _outproj(o_rw, o_da, x2, w_out.astype(BF16), attn_post_gain, g_a, ffn_pre_gain, sc_f, sh_f,
                             w_route, b_route, seq=S, tm=min(2 * tm_proj, S))

    expert_flat = route[:, 0:2].astype(jnp.int32).reshape(-1)
    n_tiles = (2 * T) // tm_moe + N_EXPERTS
    pos, row_token, tile_expert, n_used = _dispatch_plan(expert_flat, tm_moe, n_tiles)
    gather_steps_used = (n_used * tm_moe + GATHER_ROWS - 1) // GATHER_ROWS
    xs = _gather_rows(h2, row_token, gather_steps_used, tg=GATHER_ROWS)
    ys = _experts(xs, tile_expert, n_used, w_exp_gate, w_exp_up, w_exp_down, tm=tm_moe)
    out = _combine(ys, pos, route, x1, ffn_post_gain, g_f, seq=S, tm=COMBINE_ROWS)
    return out.reshape(B, S, D)


def kernel(x, c, rel_bias, w_mod, b_mod, attn_pre_gain, attn_post_gain, w_in, shift_mu, w_decay_up, decay_bias,
           w_aicl_up, aicl_bias, w_gate_up, k_k, k_a, r_k, gn_w, gn_b, lam_q1, lam_k1, lam_q2, lam_k2, subln_w,
           w_out, ffn_pre_gain, ffn_post_gain, w_coarse, b_coarse, w_fine, b_fine, w_exp_gate, w_exp_up,
           w_exp_down):
    depth = w_mod.shape[0]
    S = x.shape[1]
    tm_proj = min(256, S)
    tq = min(1024, S)
    tm_moe = 512 if S >= 4096 else 128
    for l in range(depth):
        x = _layer(x, c, rel_bias, w_mod[l], b_mod[l], attn_pre_gain[l], attn_post_gain[l], w_in[l], shift_mu[l],
                   w_decay_up[l], decay_bias[l], w_aicl_up[l], aicl_bias[l], w_gate_up[l], k_k[l], k_a[l], r_k[l],
                   gn_w[l], gn_b[l], lam_q1[l], lam_k1[l], lam_q2[l], lam_k2[l], subln_w[l], w_out[l],
                   ffn_pre_gain[l], ffn_post_gain[l], w_coarse[l], b_coarse[l], w_fine[l], b_fine[l],
                   w_exp_gate[l], w_exp_up[l], w_exp_down[l], layer_index=l, tm_proj=tm_proj, tq=tq,
                   rsub_attn=min(256, tq), tm_moe=tm_moe)
    return x
```
